```python
import math
import jax, jax.numpy as jnp
from jax import lax
import numpy as np

D_MODEL = 4096
BATCH = 16
SEQ = 256
DEPTH = 2
DEC_BATCH = 8
DEC_SEQ = 4096
PAST_LEN = 256

GRID_W = 64
HEAD_DIM = 128
ROPE_THETA = 10000.0
NORM_EPS = 1e-6
Q_BLOCK = 128

DN_HEADS = 8
DN_DK = 128
DN_DV = 128
DN_W = DN_HEADS * DN_DV
CONV_K = 3
DN_CHUNK = 64
GQA_Q_HEADS = 16
GQA_KV_HEADS = 4
GQA_W = GQA_Q_HEADS * HEAD_DIM
DIFF_HEADS = 4
DIFF_W = DIFF_HEADS * 2 * HEAD_DIM

MIX_W = DN_W + GQA_W + DIFF_W
IN_SPLITS = (DN_W, DN_W, DN_W, DN_W, 2 * DN_HEADS, 2 * DN_HEADS,
             GQA_W, GQA_KV_HEADS * HEAD_DIM, GQA_KV_HEADS * HEAD_DIM, GQA_W,
             DIFF_W, DIFF_W, DIFF_W, DIFF_W)
IN_COLS = 4 * DN_W + 4 * DN_HEADS + 2 * GQA_W + 2 * GQA_KV_HEADS * HEAD_DIM + 4 * DIFF_W

kernel_name = 'hybrid_deltanet_gqa_diffattn_prefix_dit'


def rms_norm(x, w):
    xf = x.astype(jnp.float32)
    y = xf * lax.rsqrt(jnp.mean(xf * xf, axis=-1, keepdims=True) + NORM_EPS)
    return (y * w.astype(jnp.float32)).astype(x.dtype)


def l2_norm(x):
    xf = x.astype(jnp.float32)
    return xf * lax.rsqrt(jnp.sum(xf * xf, axis=-1, keepdims=True) + NORM_EPS)


def modulation(cond, w_mod_l, b_mod_l):
    m = jax.nn.silu(cond) @ w_mod_l + b_mod_l
    return tuple(jnp.split(m, 3, axis=-1))


def split_projection(z):
    offs = np.cumsum(np.array(IN_SPLITS))[:-1].tolist()
    return jnp.split(z, offs, axis=-1)


def axial_rope_tables(t_len):
    n_rows = t_len // GRID_W
    row = jnp.repeat(jnp.arange(n_rows), GRID_W)
    col = jnp.tile(jnp.arange(GRID_W), n_rows)
    half = HEAD_DIM // 2
    inv_freq = ROPE_THETA ** (-jnp.arange(0, half, 2, dtype=jnp.float32) / half)
    pos = jnp.stack([row, col], axis=-1).astype(jnp.float32)
    ang = pos[..., None] * inv_freq
    ang = jnp.concatenate([ang, ang], axis=-1)
    return jnp.cos(ang), jnp.sin(ang)


def apply_rope(x, cos, sin):
    shp = x.shape
    xr = x.reshape(shp[:-1] + (2, HEAD_DIM // 2)).astype(jnp.float32)
    x1, x2 = jnp.split(xr, 2, axis=-1)
    rot = jnp.concatenate([-x2, x1], axis=-1)
    out = xr * cos[:, None] + rot * sin[:, None]
    return out.reshape(shp).astype(x.dtype)


def short_conv(x, w):
    t_len = x.shape[1]
    pad = CONV_K // 2
    xp = jnp.pad(x, ((0, 0), (pad, CONV_K - 1 - pad), (0, 0)))
    y = xp[:, 0:t_len] * w[0]
    for j in range(1, CONV_K):
        y = y + xp[:, j:j + t_len] * w[j]
    return jax.nn.silu(y)


def gated_delta_chunked(q, k, v, g, beta, s0):
    b, h, t_len, dk = q.shape
    dv = v.shape[-1]
    n = t_len // DN_CHUNK
    chunk = lambda t: t.reshape((b, h, n, DN_CHUNK) + t.shape[3:])
    q = chunk(q) * (dk ** -0.5)
    k, v, beta = chunk(k), chunk(v), chunk(beta)
    g = jnp.cumsum(chunk(g), axis=-1)
    idx = jnp.arange(DN_CHUNK)
    incl = idx[:, None] >= idx[None, :]
    strict = idx[:, None] > idx[None, :]
    decay = jnp.exp(jnp.where(incl, g[..., :, None] - g[..., None, :], -jnp.inf))
    kb = k * beta[..., None]
    l_mat = jnp.where(strict, jnp.einsum('bhnid,bhnjd->bhnij', kb, k) * decay, 0.0)
    eye = jnp.eye(DN_CHUNK, dtype=jnp.float32)
    rhs = jnp.concatenate([v * beta[..., None], kb * jnp.exp(g)[..., None]], axis=-1)
    uw = lax.linalg.triangular_solve(l_mat + eye, rhs, left_side=True, lower=True, unit_diagonal=True)
    u, w = uw[..., :dv], uw[..., dv:]
    qk = jnp.einsum('bhnid,bhnjd->bhnij', q, k) * decay
    g_last = g[..., -1]
    k_dec = k * jnp.exp(g_last[..., None] - g)[..., None]
    q_dec = q * jnp.exp(g)[..., None]
    xs = tuple(jnp.moveaxis(t, 2, 0) for t in (q_dec, k_dec, u, w, qk, g_last))

    def step(s, inp):
        qn, kn, un, wn, qkn, gl = inp
        v_new = un - jnp.einsum('bhck,bhkv->bhcv', wn, s)
        o = jnp.einsum('bhck,bhkv->bhcv', qn, s) + jnp.einsum('bhij,bhjv->bhiv', qkn, v_new)
        s = s * jnp.exp(gl)[..., None, None] + jnp.einsum('bhck,bhcv->bhkv', kn, v_new)
        return s, o

    s_final, o = lax.scan(step, s0, xs)
    o = jnp.moveaxis(o, 0, 2).reshape(b, h, t_len, dv)
    return o, s_final


def sweep_query_blocks(fn, q):
    b, t_len = q.shape[:2]
    nb = t_len // Q_BLOCK
    qb = jnp.moveaxis(q.reshape((b, nb, Q_BLOCK) + q.shape[2:]), 1, 0)
    out = lax.map(fn, qb)
    return jnp.moveaxis(out, 0, 1).reshape((b, t_len) + out.shape[3:])


def gqa_attend(q, k, v):
    b = q.shape[0]
    grp = GQA_Q_HEADS // GQA_KV_HEADS
    scale = HEAD_DIM ** -0.5

    def block(qb):
        qg = qb.reshape(b, Q_BLOCK, GQA_KV_HEADS, grp, HEAD_DIM)
        s = jnp.einsum('bqkgd,bskd->bkgqs', qg, k).astype(jnp.float32) * scale
        p = jax.nn.softmax(s, axis=-1)
        o = jnp.einsum('bkgqs,bskd->bqkgd', p.astype(v.dtype), v)
        return o.reshape(b, Q_BLOCK, GQA_W)

    return sweep_query_blocks(block, q)


def diff_attend(q, k, v, lam, norm_w, lambda_init):
    b = q.shape[0]
    scale = HEAD_DIM ** -0.5

    def block(qb):
        s = jnp.einsum('bqhid,bshid->bihqs', qb, k).astype(jnp.float32) * scale
        p = jax.nn.softmax(s, axis=-1)
        a = p[:, 0] - lam * p[:, 1]
        o = jnp.einsum('bhqs,bshe->bqhe', a.astype(v.dtype), v)
        o = rms_norm(o, norm_w) * (1.0 - lambda_init)
        return o.reshape(b, Q_BLOCK, DIFF_W)

    return sweep_query_blocks(block, q)


def mixer_layer(x, mod, weights, lambda_init, ctx):
    (pre_w, post_w, w_in, w_out, conv_w, a_log, dt_bias, dn_norm_w,
     q_norm_w, k_norm_w, diff_lam, diff_norm_w) = weights
    shift, scale, gate = mod
    b, t_len, _ = x.shape
    h = rms_norm(x, pre_w) * (1.0 + scale) + shift
    (dq, dk, dv, dgate, dalpha, dbeta, gq, gk, gv, ggate,
     fq, fk, fv, fgate) = split_projection(h @ w_in)

    dq, dk, dv = jnp.split(short_conv(jnp.concatenate([dq, dk, dv], axis=-1), conv_w), 3, axis=-1)
    dn_heads = lambda t: jnp.transpose(t.reshape(b, t_len, DN_HEADS, -1), (0, 2, 1, 3)).astype(jnp.float32)
    dq, dk, dv = l2_norm(dn_heads(dq)), l2_norm(dn_heads(dk)), dn_heads(dv)
    alpha = dalpha.reshape(b, t_len, 2, DN_HEADS).astype(jnp.float32)
    g = -jnp.exp(a_log.astype(jnp.float32)) * jax.nn.softplus(alpha + dt_bias.astype(jnp.float32))
    g = jnp.transpose(g, (0, 2, 3, 1))
    beta = jnp.transpose(jax.nn.sigmoid(dbeta.reshape(b, t_len, 2, DN_HEADS).astype(jnp.float32)), (0, 2, 3, 1))
    if ctx is None:
        s0 = jnp.zeros((b, 2, DN_HEADS, DN_DK, DN_DV), jnp.float32)
    else:
        s0 = ctx[4].astype(jnp.float32)
    flip = lambda t: jnp.flip(t, axis=2)
    o_f, s_f = gated_delta_chunked(dq, dk, dv, g[:, 0], beta[:, 0], s0[:, 0])
    o_b, s_b = gated_delta_chunked(flip(dq), flip(dk), flip(dv), flip(g[:, 1]), flip(beta[:, 1]), s0[:, 1])
    dn_o = jnp.transpose(o_f + flip(o_b), (0, 2, 1, 3))
    dn_out = rms_norm(dn_o, dn_norm_w).reshape(b, t_len, DN_W).astype(x.dtype) * jax.nn.silu(dgate)

    gq = rms_norm(gq.reshape(b, t_len, GQA_Q_HEADS, HEAD_DIM), q_norm_w)
    gk = rms_norm(gk.reshape(b, t_len, GQA_KV_HEADS, HEAD_DIM), k_norm_w)
    gv = gv.reshape(b, t_len, GQA_KV_HEADS, HEAD_DIM)
    fq = fq.reshape(b, t_len, DIFF_HEADS, 2, HEAD_DIM)
    fk = fk.reshape(b, t_len, DIFF_HEADS, 2, HEAD_DIM)
    fv = fv.reshape(b, t_len, DIFF_HEADS, 2 * HEAD_DIM)
    if ctx is None:
        gk_all, gv_all, fk_all, fv_all = gk, gv, fk, fv
        new_ctx = (gk, gv, fk, fv, jnp.stack([s_f, s_b], axis=1).astype(x.dtype))
    else:
        cos, sin = axial_rope_tables(t_len)
        gq = apply_rope(gq, cos, sin)
        fq = apply_rope(fq.reshape(b, t_len, 2 * DIFF_HEADS, HEAD_DIM), cos, sin).reshape(b, t_len, DIFF_HEADS, 2, HEAD_DIM)
        fk_lat = apply_rope(fk.reshape(b, t_len, 2 * DIFF_HEADS, HEAD_DIM), cos, sin).reshape(b, t_len, DIFF_HEADS, 2, HEAD_DIM)
        gk_all = jnp.concatenate([apply_rope(gk, cos, sin), ctx[0].astype(gk.dtype)], axis=1)
        gv_all = jnp.concatenate([gv, ctx[1].astype(gv.dtype)], axis=1)
        fk_all = jnp.concatenate([fk_lat, ctx[2].astype(fk.dtype)], axis=1)
        fv_all = jnp.concatenate([fv, ctx[3].astype(fv.dtype)], axis=1)
        new_ctx = None

    gqa_out = gqa_attend(gq, gk_all, gv_all) * jax.nn.silu(ggate)
    lam_p = diff_lam.astype(jnp.float32)
    lam = jnp.exp(jnp.sum(lam_p[0] * lam_p[1])) - jnp.exp(jnp.sum(lam_p[2] * lam_p[3])) + lambda_init
    diff_out = diff_attend(fq, fk_all, fv_all, lam, diff_norm_w, lambda_init) * jax.nn.silu(fgate)

    mix = jnp.concatenate([dn_out, gqa_out, diff_out], axis=-1) @ w_out
    return x + gate * rms_norm(mix, post_w), new_ctx


def setup_inputs(seed: int = 0) -> dict:
    key = jax.random.key(seed)
    ks = jax.random.split(key, 24)
    nrm = lambda k, shape, s: jax.random.normal(k, shape, jnp.float32) * s
    dt = jnp.exp(jax.random.uniform(ks[14], (DEPTH, 2, DN_HEADS), jnp.float32,
                                    minval=math.log(1e-3), maxval=math.log(1e-1)))
    return {
        'x_prompt': nrm(ks[0], (BATCH, SEQ, D_MODEL), 1.0),
        'x_sample': nrm(ks[1], (DEC_BATCH, DEC_SEQ, D_MODEL), 1.0),
        'cache_gqa_k': nrm(ks[2], (DEC_BATCH, DEPTH, PAST_LEN, GQA_KV_HEADS, HEAD_DIM), 1.0),
        'cache_gqa_v': nrm(ks[3], (DEC_BATCH, DEPTH, PAST_LEN, GQA_KV_HEADS, HEAD_DIM), 1.0),
        'cache_diff_k': nrm(ks[4], (DEC_BATCH, DEPTH, PAST_LEN, DIFF_HEADS, 2, HEAD_DIM), 1.0),
        'cache_diff_v': nrm(ks[5], (DEC_BATCH, DEPTH, PAST_LEN, DIFF_HEADS, 2 * HEAD_DIM), 1.0),
        'state_dn': nrm(ks[6], (DEC_BATCH, DEPTH, 2, DN_HEADS, DN_DK, DN_DV), 0.1),
        'c': nrm(ks[7], (DEC_BATCH, D_MODEL), 1.0),
        'c_ctx': nrm(ks[8], (D_MODEL,), 1.0),
        'w_mod': nrm(ks[9], (DEPTH, D_MODEL, 3 * D_MODEL), 0.5 * D_MODEL ** -0.5),
        'b_mod': nrm(ks[10], (DEPTH, 3 * D_MODEL), 0.01),
        'pre_norm_w': 1.0 + nrm(ks[11], (DEPTH, D_MODEL), 0.05),
        'post_norm_w': 1.0 + nrm(ks[12], (DEPTH, D_MODEL), 0.05),
        'w_in': nrm(ks[13], (DEPTH, D_MODEL, IN_COLS), D_MODEL ** -0.5),
        'w_out': nrm(ks[15], (DEPTH, MIX_W, D_MODEL), MIX_W ** -0.5),
        'dn_conv_w': nrm(ks[16], (DEPTH, CONV_K, 3 * DN_W), CONV_K ** -0.5),
        'dn_a_log': jnp.log(jax.random.uniform(ks[17], (DEPTH, 2, DN_HEADS), jnp.float32, minval=1.0, maxval=16.0)),
        'dn_dt_bias': dt + jnp.log(-jnp.expm1(-dt)),
        'dn_norm_w': 1.0 + nrm(ks[18], (DEPTH, DN_DV), 0.05),
        'gqa_q_norm_w': 1.0 + nrm(ks[19], (DEPTH, HEAD_DIM), 0.05),
        'gqa_k_norm_w': 1.0 + nrm(ks[20], (DEPTH, HEAD_DIM), 0.05),
        'diff_lambda': nrm(ks[21], (DEPTH, 4, HEAD_DIM), 0.1),
        'diff_norm_w': 1.0 + nrm(ks[22], (DEPTH, 2 * HEAD_DIM), 0.05),
    }


def reference(x_prompt, x_sample, cache_gqa_k, cache_gqa_v, cache_diff_k, cache_diff_v, state_dn,
              c, c_ctx, w_mod, b_mod, pre_norm_w, post_norm_w, w_in, w_out, dn_conv_w, dn_a_log,
              dn_dt_bias, dn_norm_w, gqa_q_norm_w, gqa_k_norm_w, diff_lambda, diff_norm_w):
    xp, xs = x_prompt, x_sample
    new_k, new_v, new_dk, new_dv, new_s = [], [], [], [], []
    for l in range(DEPTH):
        lambda_init = 0.8 - 0.6 * math.exp(-0.3 * l)
        weights = (pre_norm_w[l], post_norm_w[l], w_in[l], w_out[l], dn_conv_w[l], dn_a_log[l],
                   dn_dt_bias[l], dn_norm_w[l], gqa_q_norm_w[l], gqa_k_norm_w[l],
                   diff_lambda[l], diff_norm_w[l])
        xp, ctx_new = mixer_layer(xp, modulation(c_ctx, w_mod[l], b_mod[l]), weights, lambda_init, None)
        new_k.append(ctx_new[0])
        new_v.append(ctx_new[1])
        new_dk.append(ctx_new[2])
        new_dv.append(ctx_new[3])
        new_s.append(ctx_new[4])
        ctx_l = (cache_gqa_k[:, l], cache_gqa_v[:, l], cache_diff_k[:, l], cache_diff_v[:, l], state_dn[:, l])
        xs, _ = mixer_layer(xs, modulation(c[:, None, :], w_mod[l], b_mod[l]), weights, lambda_init, ctx_l)
    return (xp, xs, jnp.stack(new_k, axis=1), jnp.stack(new_v, axis=1), jnp.stack(new_dk, axis=1),
            jnp.stack(new_dv, axis=1), jnp.stack(new_s, axis=1))
```

```python
import functools
import math

import numpy as np
import jax
import jax.numpy as jnp
from jax import lax
from jax.experimental import pallas as pl
from jax.experimental.pallas import tpu as pltpu

F32 = jnp.float32
BF16 = jnp.bfloat16

HEAD_DIM = 128
GRID_W = 64
ROPE_THETA = 10000.0
NORM_EPS = 1e-6
DN_HEADS = 8
DN_CHUNK = 64
CONV_K = 3
GQA_Q_HEADS = 16
GQA_KV_HEADS = 4
GQA_GROUP = GQA_Q_HEADS // GQA_KV_HEADS
DIFF_HEADS = 4
DN_W = DN_HEADS * HEAD_DIM
GQA_W = GQA_Q_HEADS * HEAD_DIM
GQA_KV_W = GQA_KV_HEADS * HEAD_DIM
DIFF_W = DIFF_HEADS * 2 * HEAD_DIM
MIX_W = DN_W + GQA_W + DIFF_W
N_GATE_COLS = 4 * DN_HEADS

OFF_DQ, OFF_DK, OFF_DV, OFF_DG = 0, 8, 16, 24
OFF_GQ, OFF_GK, OFF_GV, OFF_GG = 32, 48, 52, 56
OFF_FQ, OFF_FK, OFF_FV, OFF_FG = 72, 80, 88, 96
Z_BLOCKS = 104
Z_COLS = Z_BLOCKS * HEAD_DIM

VMEM_LIMIT_BYTES = 56 * 1024 * 1024
MOD_ROWS = 16


def _params(*sem):
    return pltpu.CompilerParams(dimension_semantics=sem, vmem_limit_bytes=VMEM_LIMIT_BYTES)


def _silu(x):
    return x * jax.nn.sigmoid(x)


def _bdot(a, b):
    return jnp.dot(a.astype(BF16), b.astype(BF16), preferred_element_type=F32)


def _bdot_nt(a, b):
    return lax.dot_general(a.astype(BF16), b.astype(BF16), (((1,), (1,)), ((), ())),
                           preferred_element_type=F32)


def _bdot_tn(a, b):
    return lax.dot_general(a.astype(BF16), b.astype(BF16), (((0,), (0,)), ((), ())),
                           preferred_element_type=F32)


def _mod_kernel(c_ref, w_ref, b_ref, o_ref):
    o_ref[...] = _bdot(_silu(c_ref[...]), w_ref[...]) + b_ref[...]


def _modulation(cond, w_mod, b_mod):
    n_layers, d, n = w_mod.shape
    tn = min(512, n)
    return pl.pallas_call(
        _mod_kernel,
        grid=(n_layers, n // tn),
        in_specs=[
            pl.BlockSpec((MOD_ROWS, d), lambda l, j: (0, 0)),
            pl.BlockSpec((None, d, tn), lambda l, j: (l, 0, j)),
            pl.BlockSpec((None, 1, tn), lambda l, j: (l, 0, j)),
        ],
        out_specs=pl.BlockSpec((None, MOD_ROWS, tn), lambda l, j: (l, 0, j)),
        out_shape=jax.ShapeDtypeStruct((n_layers, MOD_ROWS, n), F32),
        compiler_params=_params("arbitrary", "arbitrary"),
        name="modulation",
    )(cond, w_mod, b_mod.reshape(n_layers, 1, n))


def _inproj_kernel(x_ref, sh_ref, sc_ref, pw_ref, w_ref, wab_ref, z_ref, ab_ref, h_scr):
    @pl.when(pl.program_id(1) == 0)
    def _():
        x = x_ref[...]
        y = x * lax.rsqrt(jnp.mean(x * x, axis=-1, keepdims=True) + NORM_EPS) * pw_ref[...]
        h = (y * (1.0 + sc_ref[...]) + sh_ref[...]).astype(BF16)
        h_scr[...] = h
        ab_ref[...] = jnp.dot(h, wab_ref[...], preferred_element_type=F32)

    z_ref[...] = jnp.dot(h_scr[...], w_ref[...], preferred_element_type=F32).astype(z_ref.dtype)


def _in_projection(x, mod, mod_row0, rows_per_mod, layer, pre_w, w_main, w_ab):
    m, d = x.shape
    tm = min(512, rows_per_mod)
    tn = 1024
    assert rows_per_mod % tm == 0 and m % tm == 0 and Z_COLS % tn == 0
    tiles_per_mod = rows_per_mod // tm
    base = layer * MOD_ROWS + mod_row0

    def mod_map(part):
        return lambda i, j: (base + i // tiles_per_mod, 0, part)

    return pl.pallas_call(
        _inproj_kernel,
        grid=(m // tm, Z_COLS // tn),
        in_specs=[
            pl.BlockSpec((tm, d), lambda i, j: (i, 0)),
            pl.BlockSpec((None, 1, d), mod_map(0)),
            pl.BlockSpec((None, 1, d), mod_map(1)),
            pl.BlockSpec((None, 1, d), lambda i, j: (layer, 0, 0)),
            pl.BlockSpec((None, d, tn), lambda i, j: (layer, 0, j)),
            pl.BlockSpec((None, d, 128), lambda i, j: (layer, 0, 0)),
        ],
        out_specs=[
            pl.BlockSpec((tm, tn), lambda i, j: (i, j)),
            pl.BlockSpec((tm, 128), lambda i, j: (i, 0)),
        ],
        out_shape=[jax.ShapeDtypeStruct((m, Z_COLS), BF16), jax.ShapeDtypeStruct((m, 128), F32)],
        scratch_shapes=[pltpu.VMEM((tm, d), BF16)],
        compiler_params=_params("arbitrary", "arbitrary"),
        name="in_projection",
    )(x, mod, mod, pre_w, w_main, w_ab)


def _rope_tables(t_len):
    half = HEAD_DIM // 2
    t = np.arange(t_len)
    pos = np.stack([t // GRID_W, t % GRID_W], axis=-1).astype(np.float64)
    inv_freq = ROPE_THETA ** (-np.arange(0, half, 2, dtype=np.float64) / half)
    ang = pos[..., None] * inv_freq
    ang = np.concatenate([ang, ang], axis=-1).reshape(t_len, HEAD_DIM)
    first = (np.arange(HEAD_DIM) % half) < half // 2
    cos = np.cos(ang)
    sin = np.sin(ang)
    sin_up = np.where(first, -sin, 0.0)
    sin_dn = np.where(first, 0.0, sin)
    return tuple(jnp.asarray(a, F32) for a in (cos, sin_up, sin_dn))


def _rope(x, cos, sin_up, sin_dn):
    q = HEAD_DIM // 4
    return x * cos + pltpu.roll(x, HEAD_DIM - q, 1) * sin_up + pltpu.roll(x, q, 1) * sin_dn


def _rms(x, w):
    return x * lax.rsqrt(jnp.mean(x * x, axis=-1, keepdims=True) + NORM_EPS) * w


def _keyprep_kernel(*refs, rope, n_fk):
    if rope:
        gk_ref, fk_ref, kw_ref, cos_ref, su_ref, sd_ref, gko_ref, fko_ref = refs
        tabs = (cos_ref[...], su_ref[...], sd_ref[...])
    else:
        gk_ref, kw_ref, gko_ref = refs
    kw = kw_ref[...]
    for j in range(GQA_KV_HEADS):
        sl = slice(j * HEAD_DIM, (j + 1) * HEAD_DIM)
        k = _rms(gk_ref[:, sl].astype(F32), kw)
        if rope:
            k = _rope(k, *tabs)
        gko_ref[:, sl] = k.astype(gko_ref.dtype)
    if rope:
        for j in range(n_fk):
            sl = slice(j * HEAD_DIM, (j + 1) * HEAD_DIM)
            fko_ref[:, sl] = _rope(fk_ref[:, sl].astype(F32), *tabs).astype(fko_ref.dtype)


def _key_prep(z, t_len, layer, k_norm_w, rope):
    m = z.shape[0]
    tm = min(256, t_len)
    tiles_per_seq = t_len // tm
    kw_spec = pl.BlockSpec((None, 1, HEAD_DIM), lambda i: (layer, 0, 0))
    gk_spec = pl.BlockSpec((tm, GQA_KV_W), lambda i: (i, OFF_GK * HEAD_DIM // GQA_KV_W))
    if rope:
        tabs = _rope_tables(t_len)
        tab_spec = pl.BlockSpec((tm, HEAD_DIM), lambda i: (i % tiles_per_seq, 0))
        return pl.pallas_call(
            functools.partial(_keyprep_kernel, rope=True, n_fk=2 * DIFF_HEADS),
            grid=(m // tm,),
            in_specs=[gk_spec, pl.BlockSpec((tm, DIFF_W), lambda i: (i, OFF_FK * HEAD_DIM // DIFF_W)),
                      kw_spec, tab_spec, tab_spec, tab_spec],
            out_specs=[pl.BlockSpec((tm, GQA_KV_W), lambda i: (i, 0)),
                       pl.BlockSpec((tm, DIFF_W), lambda i: (i, 0))],
            out_shape=[jax.ShapeDtypeStruct((m, GQA_KV_W), BF16), jax.ShapeDtypeStruct((m, DIFF_W), BF16)],
            compiler_params=_params("arbitrary"),
            name="key_prep_rope",
        )(z, z, k_norm_w, *tabs)
    return pl.pallas_call(
        functools.partial(_keyprep_kernel, rope=False, n_fk=0),
        grid=(m // tm,),
        in_specs=[gk_spec, kw_spec],
        out_specs=pl.BlockSpec((tm, GQA_KV_W), lambda i: (i, 0)),
        out_shape=jax.ShapeDtypeStruct((m, GQA_KV_W), F32),
        compiler_params=_params("arbitrary"),
        name="key_prep",
    )(z, k_norm_w)


def _softmax_step(q, k, v, m_ref, l_ref, acc_ref):
    s = _bdot_nt(q, k)
    m_prev = m_ref[...]
    m_new = jnp.maximum(m_prev, jnp.max(s, axis=-1, keepdims=True))
    alpha = jnp.exp(m_prev - m_new)
    p = jnp.exp(s - m_new[:, :1])
    l_ref[...] = alpha * l_ref[...] + jnp.sum(p, axis=-1, keepdims=True)
    acc_ref[...] = acc_ref[...] * alpha[:, :1] + _bdot(p, v)
    m_ref[...] = m_new


def _attend(q, k_ref, v_ref, ck_ref, cv_ref, m_ref, l_ref, acc_ref, tk):
    m_ref[...] = jnp.full(m_ref.shape, -jnp.inf, F32)
    l_ref[...] = jnp.zeros(l_ref.shape, F32)
    acc_ref[...] = jnp.zeros(acc_ref.shape, F32)
    n_chunks = k_ref.shape[0] // tk

    def body(c, carry):
        r0 = pl.multiple_of(c * tk, tk)
        _softmax_step(q, k_ref[pl.ds(r0, tk), :], v_ref[pl.ds(r0, tk), :], m_ref, l_ref, acc_ref)
        return carry

    lax.fori_loop(0, n_chunks, body, 0)
    if ck_ref is not None:
        _softmax_step(q, ck_ref[...], cv_ref[...], m_ref, l_ref, acc_ref)
    return acc_ref[...] / l_ref[...][:, :1]


def _gqa_kernel(*refs, rope, cached, tq, tk):
    refs = list(refs)
    q_ref, g_ref, k_ref, v_ref, qw_ref = refs[:5]
    pos = 5
    ck_ref = cv_ref = None
    if cached:
        ck_ref, cv_ref = refs[pos:pos + 2]
        pos += 2
    if rope:
        tabs = tuple(r[...] for r in refs[pos:pos + 3])
        pos += 3
    o_ref, q_scr, m_ref, l_ref, acc_ref = refs[pos:]
    qw = qw_ref[...]
    scale = HEAD_DIM ** -0.5
    for j in range(GQA_GROUP):
        qj = _rms(q_ref[:, j * HEAD_DIM:(j + 1) * HEAD_DIM].astype(F32), qw)
        if rope:
            qj = _rope(qj, *tabs)
        q_scr[j * tq:(j + 1) * tq, :] = (qj * scale).astype(BF16)
    o = _attend(q_scr[...], k_ref, v_ref, ck_ref, cv_ref, m_ref, l_ref, acc_ref, tk)
    for j in range(GQA_GROUP):
        sl = slice(j * HEAD_DIM, (j + 1) * HEAD_DIM)
        o_ref[:, sl] = (o[j * tq:(j + 1) * tq, :] * _silu(g_ref[:, sl].astype(F32))).astype(o_ref.dtype)


def _gqa_attention(z, keys, t_len, layer, q_norm_w, cache_k, cache_v, rope):
    m = z.shape[0]
    n_seq = m // t_len
    tq = min(256, t_len)
    tk = min(512, t_len)
    nq = t_len // tq
    cached = cache_k is not None
    qcol = OFF_GQ * HEAD_DIM // (GQA_GROUP * HEAD_DIM)
    gcol = OFF_GG * HEAD_DIM // (GQA_GROUP * HEAD_DIM)
    in_specs = [
        pl.BlockSpec((tq, GQA_GROUP * HEAD_DIM), lambda b, g, i: (b * nq + i, qcol + g)),
        pl.BlockSpec((tq, GQA_GROUP * HEAD_DIM), lambda b, g, i: (b * nq + i, gcol + g)),
        pl.BlockSpec((t_len, HEAD_DIM), lambda b, g, i: (b, g)),
        pl.BlockSpec((t_len, HEAD_DIM), lambda b, g, i: (b, OFF_GV + g)),
        pl.BlockSpec((None, 1, HEAD_DIM), lambda b, g, i: (layer, 0, 0)),
    ]
    args = [z, z, keys, z, q_norm_w]
    if cached:
        p_len = cache_k.shape[2]
        spec = pl.BlockSpec((None, None, p_len, HEAD_DIM), lambda b, g, i: (b, layer, 0, g))
        in_specs += [spec, spec]
        args += [cache_k, cache_v]
    if rope:
        spec = pl.BlockSpec((tq, HEAD_DIM), lambda b, g, i: (i, 0))
        in_specs += [spec, spec, spec]
        args += list(_rope_tables(t_len))
    rows = GQA_GROUP * tq
    return pl.pallas_call(
        functools.partial(_gqa_kernel, rope=rope, cached=cached, tq=tq, tk=tk),
        grid=(n_seq, GQA_KV_HEADS, nq),
        in_specs=in_specs,
        out_specs=pl.BlockSpec((tq, GQA_GROUP * HEAD_DIM), lambda b, g, i: (b * nq + i, g)),
        out_shape=jax.ShapeDtypeStruct((m, GQA_W), BF16),
        scratch_shapes=[pltpu.VMEM((rows, HEAD_DIM), BF16), pltpu.VMEM((rows, 128), F32),
                        pltpu.VMEM((rows, 128), F32), pltpu.VMEM((rows, HEAD_DIM), F32)],
        compiler_params=_params("arbitrary", "arbitrary", "arbitrary"),
        name="gqa_attention",
    )(*args)


def _diff_kernel(*refs, rope, cached, tk, lambda_init):
    refs = list(refs)
    q_ref, g_ref, k_ref, v_ref, lam_ref, nw_ref = refs[:6]
    pos = 6
    ck_ref = cv_ref = None
    if cached:
        ck_ref, cv_ref = refs[pos:pos + 2]
        pos += 2
    if rope:
        tabs = tuple(r[...] for r in refs[pos:pos + 3])
        pos += 3
    o_ref, m_ref, l_ref, acc_ref = refs[pos:]
    scale = HEAD_DIM ** -0.5
    lp = lam_ref[...]
    lam = (jnp.exp(jnp.sum(lp[0:1] * lp[1:2], axis=-1, keepdims=True))
           - jnp.exp(jnp.sum(lp[2:3] * lp[3:4], axis=-1, keepdims=True)) + lambda_init)
    outs = []
    for i in range(2):
        sl = slice(i * HEAD_DIM, (i + 1) * HEAD_DIM)
        qi = q_ref[:, sl].astype(F32)
        if rope:
            qi = _rope(qi, *tabs)
        qi = (qi * scale).astype(BF16)
        outs.append(_attend(qi, k_ref.at[:, sl], v_ref, None if ck_ref is None else ck_ref.at[:, sl],
                            cv_ref, m_ref, l_ref, acc_ref, tk))
    o = outs[0] - lam * outs[1]
    o = _rms(o, nw_ref[...]) * (1.0 - lambda_init)
    o_ref[...] = (o * _silu(g_ref[...].astype(F32))).astype(o_ref.dtype)


def _diff_attention(z, keys, key_col0, t_len, layer, lambda_init, diff_lambda, diff_norm_w,
                    cache_k, cache_v, rope):
    m = z.shape[0]
    n_seq = m // t_len
    tq = min(512, t_len)
    tk = min(512, t_len)
    nq = t_len // tq
    hw = 2 * HEAD_DIM
    cached = cache_k is not None
    in_specs = [
        pl.BlockSpec((tq, hw), lambda b, h, i: (b * nq + i, OFF_FQ // 2 + h)),
        pl.BlockSpec((tq, hw), lambda b, h, i: (b * nq + i, OFF_FG // 2 + h)),
        pl.BlockSpec((t_len, hw), lambda b, h, i: (b, key_col0 + h)),
        pl.BlockSpec((t_len, hw), lambda b, h, i: (b, OFF_FV // 2 + h)),
        pl.BlockSpec((None, 4, HEAD_DIM), lambda b, h, i: (layer, 0, 0)),
        pl.BlockSpec((None, 1, hw), lambda b, h, i: (layer, 0, 0)),
    ]
    args = [z, z, keys, z, diff_lambda, diff_norm_w]
    if cached:
        p_len = cache_k.shape[2]
        spec = pl.BlockSpec((None, None, p_len, hw), lambda b, h, i: (b, layer, 0, h))
        in_specs += [spec, spec]
        args += [cache_k, cache_v]
    if rope:
        spec = pl.BlockSpec((tq, HEAD_DIM), lambda b, h, i: (i, 0))
        in_specs += [spec, spec, spec]
        args += list(_rope_tables(t_len))
    return pl.pallas_call(
        functools.partial(_diff_kernel, rope=rope, cached=cached, tk=tk, lambda_init=lambda_init),
        grid=(n_seq, DIFF_HEADS, nq),
        in_specs=in_specs,
        out_specs=pl.BlockSpec((tq, hw), lambda b, h, i: (b * nq + i, h)),
        out_shape=jax.ShapeDtypeStruct((m, DIFF_W), BF16),
        scratch_shapes=[pltpu.VMEM((tq, 128), F32), pltpu.VMEM((tq, 128), F32), pltpu.VMEM((tq, hw), F32)],
        compiler_params=_params("arbitrary", "arbitrary", "arbitrary"),
        name="diff_attention",
    )(*args)


def _mm(a, b):
    return jnp.dot(a, b, preferred_element_type=F32, precision=lax.Precision.HIGHEST)


def _unit_tri_inverse(l_mat):
    n = l_mat.shape[0]
    eye = (lax.broadcasted_iota(jnp.int32, (n, n), 0) == lax.broadcasted_iota(jnp.int32, (n, n), 1)).astype(F32)
    t = eye - l_mat
    p = _mm(l_mat, l_mat)
    steps = int(math.log2(n)) - 1
    for s in range(steps):
        t = t + _mm(t, p)
        if s + 1 < steps:
            p = _mm(p, p)
    return t


def _chunk_cumsum(x, reverse):
    n = x.shape[0]
    row = lax.broadcasted_iota(jnp.int32, x.shape, 0)
    s = 1
    while s < n:
        if reverse:
            x = x + jnp.where(row < n - s, pltpu.roll(x, n - s, 0), 0.0)
        else:
            x = x + jnp.where(row >= s, pltpu.roll(x, s, 0), 0.0)
        s *= 2
    return x


def _dn_kernel(*refs, t_len, has_state):
    refs = list(refs)
    (zq_ref, zk_ref, zv_ref, zg_ref, ab_ref, cwq_ref, cwk_ref, cwv_ref,
     alog_ref, dtb_ref, nw_ref) = refs[:11]
    pos = 11
    s0_ref = None
    if has_state:
        s0_ref = refs[pos]
        pos += 1
    o_ref, sfin_ref = refs[pos:pos + 2]
    (xq_s, xk_s, xv_s, u_s, w_s, qd_s, kd_s, qk_s, et_s, s_s, o_s) = refs[pos + 2:]
    c_len = DN_CHUNK
    n_chunks = t_len // c_len
    pad = 8
    head = pl.program_id(1)

    zero_pad = jnp.zeros((pad, HEAD_DIM), F32)
    for src, dst in ((zq_ref, xq_s), (zk_ref, xk_s), (zv_ref, xv_s)):
        dst[0:pad, :] = zero_pad
        dst[pad + t_len:2 * pad + t_len, :] = zero_pad

    def fill(c, carry):
        r0 = pl.multiple_of(c * c_len, c_len)
        for src, dst in ((zq_ref, xq_s), (zk_ref, xk_s), (zv_ref, xv_s)):
            dst[pl.ds(r0 + pad, c_len), :] = src[pl.ds(r0, c_len), :].astype(F32)
        return carry

    lax.fori_loop(0, n_chunks, fill, 0)

    def conv(x_s, cw_ref, r0):
        win = x_s[pl.ds(r0, c_len + 2 * pad), :]
        n = c_len + 2 * pad
        prev = pltpu.roll(win, 1, 0)[pad:pad + c_len]
        nxt = pltpu.roll(win, n - 1, 0)[pad:pad + c_len]
        cw = cw_ref[...]
        y = prev * cw[0:1] + win[pad:pad + c_len] * cw[1:2] + nxt * cw[2:3]
        return _silu(y)

    def l2n(x):
        return x * lax.rsqrt(jnp.sum(x * x, axis=-1, keepdims=True) + NORM_EPS)

    lane = lax.broadcasted_iota(jnp.int32, (c_len, 128), 1)
    row64 = lax.broadcasted_iota(jnp.int32, (c_len, c_len), 0)
    col64 = lax.broadcasted_iota(jnp.int32, (c_len, c_len), 1)
    neg_a = -jnp.exp(alog_ref[...])
    dtb = dtb_ref[...]

    def pick(x, col):
        return jnp.broadcast_to(jnp.sum(jnp.where(lane == col, x, 0.0), axis=-1, keepdims=True),
                                (c_len, HEAD_DIM))

    def precompute(c, carry):
        r0 = pl.multiple_of(c * c_len, c_len)
        q = l2n(conv(xq_s, cwq_ref, r0)) * (HEAD_DIM ** -0.5)
        k = l2n(conv(xk_s, cwk_ref, r0))
        v = conv(xv_s, cwv_ref, r0)
        ab = ab_ref[pl.ds(r0, c_len), :]
        x = ab + dtb
        g_all = neg_a * (jnp.maximum(x, 0.0) + jnp.log1p(jnp.exp(-jnp.abs(x))))
        b_all = jax.nn.sigmoid(ab)
        kk = _bdot_nt(k, k)
        qk = _bdot_nt(q, k)
        for d in range(2):
            g = pick(g_all, d * DN_HEADS + head)
            beta = pick(b_all, 2 * DN_HEADS + d * DN_HEADS + head)
            gc = _chunk_cumsum(g, reverse=(d == 1))
            tot = jnp.sum(g, axis=0, keepdims=True)
            gc_t = jnp.transpose(gc)[0:c_len, :]
            incl = (row64 >= col64) if d == 0 else (row64 <= col64)
            strict = (row64 > col64) if d == 0 else (row64 < col64)
            decay = jnp.where(incl, jnp.exp(jnp.where(incl, gc[:, 0:c_len] - gc_t, 0.0)), 0.0)
            l_mat = jnp.where(strict, kk * beta[:, 0:c_len] * decay, 0.0)
            t_inv = _unit_tri_inverse(l_mat)
            kb = k * beta
            rhs = jnp.concatenate([v * beta, kb * jnp.exp(gc)], axis=-1)
            uw = _mm(t_inv, rhs)
            u_s[d, pl.ds(r0, c_len), :] = uw[:, 0:HEAD_DIM]
            w_s[d, pl.ds(r0, c_len), :] = uw[:, HEAD_DIM:].astype(BF16)
            qd_s[d, pl.ds(r0, c_len), :] = (q * jnp.exp(gc)).astype(BF16)
            kd_s[d, pl.ds(r0, c_len), :] = (k * jnp.exp(tot - gc)).astype(BF16)
            qk_s[d, pl.ds(r0, c_len), :] = (qk * decay).astype(BF16)
            et_s[d, c] = jnp.broadcast_to(jnp.exp(tot), (8, HEAD_DIM))
        return carry

    lax.fori_loop(0, n_chunks, precompute, 0)

    if has_state:
        s_s[...] = s0_ref[...].astype(F32)
    else:
        s_s[...] = jnp.zeros(s_s.shape, F32)

    def scan(i, carry):
        for d in range(2):
            c = i if d == 0 else n_chunks - 1 - i
            r0 = pl.multiple_of(c * c_len, c_len)
            s = s_s[d]
            v_new = u_s[d, pl.ds(r0, c_len), :] - _bdot(w_s[d, pl.ds(r0, c_len), :], s)
            o = _bdot(qd_s[d, pl.ds(r0, c_len), :], s) + _bdot(qk_s[d, pl.ds(r0, c_len), :], v_new)
            s_s[d] = s * et_s[d, c][0:1, :] + _bdot_tn(kd_s[d, pl.ds(r0, c_len), :], v_new)
            o_s[d, pl.ds(r0, c_len), :] = o
        return carry

    lax.fori_loop(0, n_chunks, scan, 0)
    sfin_ref[...] = s_s[...].astype(sfin_ref.dtype)

    nw = nw_ref[...]

    def finish(c, carry):
        r0 = pl.multiple_of(c * c_len, c_len)
        o = o_s[0, pl.ds(r0, c_len), :] + o_s[1, pl.ds(r0, c_len), :]
        o = _rms(o, nw) * _silu(zg_ref[pl.ds(r0, c_len), :].astype(F32))
        o_ref[pl.ds(r0, c_len), :] = o.astype(o_ref.dtype)
        return carry

    lax.fori_loop(0, n_chunks, finish, 0)


def _delta_net(z, ab, t_len, layer, conv_w, a_log_row, dt_bias_row, dn_norm_w, state):
    m = z.shape[0]
    n_seq = m // t_len
    n_chunks = t_len // DN_CHUNK
    has_state = state is not None

    def zspec(off):
        return pl.BlockSpec((t_len, HEAD_DIM), lambda b, h: (b, off + h))

    def cwspec(off):
        return pl.BlockSpec((None, CONV_K, HEAD_DIM), lambda b, h: (layer, 0, off + h))

    row_spec = pl.BlockSpec((None, 1, 128), lambda b, h: (layer, 0, 0))
    in_specs = [zspec(OFF_DQ), zspec(OFF_DK), zspec(OFF_DV), zspec(OFF_DG),
                pl.BlockSpec((t_len, 128), lambda b, h: (b, 0)),
                cwspec(0), cwspec(DN_HEADS), cwspec(2 * DN_HEADS),
                row_spec, row_spec, row_spec]
    args = [z, z, z, z, ab, conv_w, conv_w, conv_w, a_log_row, dt_bias_row, dn_norm_w]
    if has_state:
        in_specs.append(pl.BlockSpec((None, None, 2, None, HEAD_DIM, HEAD_DIM),
                                     lambda b, h: (b, layer, 0, h, 0, 0)))
        args.append(state)
    tt = t_len
    scratch = [pltpu.VMEM((tt + 16, HEAD_DIM), F32)] * 3 + [
        pltpu.VMEM((2, tt, HEAD_DIM), F32),
        pltpu.VMEM((2, tt, HEAD_DIM), BF16),
        pltpu.VMEM((2, tt, HEAD_DIM), BF16),
        pltpu.VMEM((2, tt, HEAD_DIM), BF16),
        pltpu.VMEM((2, tt, DN_CHUNK), BF16),
        pltpu.VMEM((2, n_chunks, 8, HEAD_DIM), F32),
        pltpu.VMEM((2, HEAD_DIM, HEAD_DIM), F32),
        pltpu.VMEM((2, tt, HEAD_DIM), F32),
    ]
    return pl.pallas_call(
        functools.partial(_dn_kernel, t_len=t_len, has_state=has_state),
        grid=(n_seq, DN_HEADS),
        in_specs=in_specs,
        out_specs=[pl.BlockSpec((t_len, HEAD_DIM), lambda b, h: (b, h)),
                   pl.BlockSpec((None, 2, None, HEAD_DIM, HEAD_DIM), lambda b, h: (b, 0, h, 0, 0))],
        out_shape=[jax.ShapeDtypeStruct((m, DN_W), BF16),
                   jax.ShapeDtypeStruct((n_seq, 2, DN_HEADS, HEAD_DIM, HEAD_DIM), F32)],
        scratch_shapes=scratch,
        compiler_params=_params("arbitrary", "arbitrary"),
        name="delta_net",
    )(*args)


def _outproj_kernel(a1_ref, a2_ref, a3_ref, w_ref, o_ref):
    acc = jnp.dot(a1_ref[...], w_ref[0:DN_W, :], preferred_element_type=F32)
    acc += jnp.dot(a2_ref[...], w_ref[DN_W:DN_W + GQA_W, :], preferred_element_type=F32)
    acc += jnp.dot(a3_ref[...], w_ref[DN_W + GQA_W:MIX_W, :], preferred_element_type=F32)
    o_ref[...] = acc.astype(o_ref.dtype)


def _out_projection(dn, gqa, diff, w_out, layer):
    m = dn.shape[0]
    d = w_out.shape[-1]
    tm = min(1024, m)
    tn = min(1024, d)
    return pl.pallas_call(
        _outproj_kernel,
        grid=(m // tm, d // tn),
        in_specs=[pl.BlockSpec((tm, DN_W), lambda i, j: (i, 0)),
                  pl.BlockSpec((tm, GQA_W), lambda i, j: (i, 0)),
                  pl.BlockSpec((tm, DIFF_W), lambda i, j: (i, 0)),
                  pl.BlockSpec((None, MIX_W, tn), lambda i, j: (layer, 0, j))],
        out_specs=pl.BlockSpec((tm, tn), lambda i, j: (i, j)),
        out_shape=jax.ShapeDtypeStruct((m, d), BF16),
        compiler_params=_params("arbitrary", "arbitrary"),
        name="out_projection",
    )(dn, gqa, diff, w_out)


def _residual_kernel(x_ref, mix_ref, g_ref, pw_ref, o_ref):
    o_ref[...] = x_ref[...] + g_ref[...] * _rms(mix_ref[...].astype(F32), pw_ref[...])


def _post_residual(x, mix, mod, mod_row0, rows_per_mod, layer, post_w):
    m, d = x.shape
    tm = min(256, rows_per_mod)
    tiles_per_mod = rows_per_mod // tm
    base = layer * MOD_ROWS + mod_row0
    return pl.pallas_call(
        _residual_kernel,
        grid=(m // tm,),
        in_specs=[pl.BlockSpec((tm, d), lambda i: (i, 0)),
                  pl.BlockSpec((tm, d), lambda i: (i, 0)),
                  pl.BlockSpec((None, 1, d), lambda i: (base + i // tiles_per_mod, 0, 2)),
                  pl.BlockSpec((None, 1, d), lambda i: (layer, 0, 0))],
        out_specs=pl.BlockSpec((tm, d), lambda i: (i, 0)),
        out_shape=jax.ShapeDtypeStruct((m, d), F32),
        compiler_params=_params("arbitrary"),
        name="post_residual",
    )(x, mix, mod, post_w)


def _mixer_layer(x, t_len, mod, mod_row0, rows_per_mod, layer, lambda_init, wts, caches):
    (pre_w, post_w, w_main, w_ab, w_out, conv_w, a_log_row, dt_bias_row, dn_norm_w,
     q_norm_w, k_norm_w, diff_lambda, diff_norm_w) = wts
    latent = caches is not None
    z, ab = _in_projection(x, mod, mod_row0, rows_per_mod, layer, pre_w, w_main, w_ab)
    if latent:
        cgk, cgv, cfk, cfv, state = caches
        gkeys, fkeys = _key_prep(z, t_len, layer, k_norm_w, rope=True)
        fkey_col0 = 0
    else:
        cgk = cgv = cfk = cfv = state = None
        gkeys = _key_prep(z, t_len, layer, k_norm_w, rope=False)
        fkeys, fkey_col0 = z, OFF_FK // 2
    dn, s_fin = _delta_net(z, ab, t_len, layer, conv_w, a_log_row, dt_bias_row, dn_norm_w, state)
    gqa = _gqa_attention(z, gkeys, t_len, layer, q_norm_w, cgk, cgv, rope=latent)
    diff = _diff_attention(z, fkeys, fkey_col0, t_len, layer, lambda_init, diff_lambda, diff_norm_w,
                           cfk, cfv, rope=latent)
    mix = _out_projection(dn, gqa, diff, w_out, layer)
    y = _post_residual(x, mix, mod, mod_row0, rows_per_mod, layer, post_w)
    return y, z, gkeys, s_fin


def kernel(x_prompt, x_sample, cache_gqa_k, cache_gqa_v, cache_diff_k, cache_diff_v, state_dn, c, c_ctx,
           w_mod, b_mod, pre_norm_w, post_norm_w, w_in, w_out, dn_conv_w, dn_a_log, dn_dt_bias, dn_norm_w,
           gqa_q_norm_w, gqa_k_norm_w, diff_lambda, diff_norm_w):
    n_ctx, t_ctx, d = x_prompt.shape
    n_lat, t_lat, _ = x_sample.shape
    depth = w_in.shape[0]
    p_len = cache_gqa_k.shape[2]
    assert n_lat + 1 <= MOD_ROWS

    n_dn = 4 * DN_W
    w_main = jnp.concatenate([w_in[:, :, :n_dn], w_in[:, :, n_dn + N_GATE_COLS:]], axis=-1).astype(BF16)
    w_ab = jnp.pad(w_in[:, :, n_dn:n_dn + N_GATE_COLS], ((0, 0), (0, 0), (0, 128 - N_GATE_COLS))).astype(BF16)
    w_out_b = w_out.astype(BF16)
    row128 = lambda a: jnp.pad(a.reshape(depth, 1, -1), ((0, 0), (0, 0), (0, 128 - a[0].size)))
    a_log_row = row128(dn_a_log)
    dt_bias_row = row128(dn_dt_bias)
    r3 = lambda a: a.reshape(depth, 1, a.shape[-1])

    cond = jnp.concatenate([c_ctx[None, :], c, jnp.zeros((MOD_ROWS - 1 - n_lat, d), F32)], axis=0)
    mod = _modulation(cond, w_mod, b_mod).reshape(depth * MOD_ROWS, 1, 3 * d)

    cgk = cache_gqa_k.reshape(n_lat, depth, p_len, GQA_KV_W)
    cgv = cache_gqa_v.reshape(n_lat, depth, p_len, GQA_KV_W)
    cfk = cache_diff_k.reshape(n_lat, depth, p_len, DIFF_W)
    cfv = cache_diff_v.reshape(n_lat, depth, p_len, DIFF_W)

    xp = x_prompt.reshape(n_ctx * t_ctx, d)
    xs = x_sample.reshape(n_lat * t_lat, d)
    new_k, new_v, new_dk, new_dv, new_s = [], [], [], [], []
    for l in range(depth):
        lambda_init = 0.8 - 0.6 * math.exp(-0.3 * l)
        wts = (r3(pre_norm_w), r3(post_norm_w), w_main, w_ab, w_out_b, dn_conv_w, a_log_row, dt_bias_row,
               r3(dn_norm_w), r3(gqa_q_norm_w), r3(gqa_k_norm_w), diff_lambda, r3(diff_norm_w))
        xp, z, gk, s_fin = _mixer_layer(xp, t_ctx, mod, 0, n_ctx * t_ctx, l, lambda_init, wts, None)
        zc = lambda off, n: z[:, off * HEAD_DIM:off * HEAD_DIM + n].astype(F32)
        new_k.append(gk.reshape(n_ctx, t_ctx, GQA_KV_HEADS, HEAD_DIM))
        new_v.append(zc(OFF_GV, GQA_KV_W).reshape(n_ctx, t_ctx, GQA_KV_HEADS, HEAD_DIM))
        new_dk.append(zc(OFF_FK, DIFF_W).reshape(n_ctx, t_ctx, DIFF_HEADS, 2, HEAD_DIM))
        new_dv.append(zc(OFF_FV, DIFF_W).reshape(n_ctx, t_ctx, DIFF_HEADS, 2 * HEAD_DIM))
        new_s.append(s_fin)
        xs, _, _, _ = _mixer_layer(xs, t_lat, mod, 1, t_lat, l, lambda_init, wts,
                                   (cgk, cgv, cfk, cfv, state_dn))
    return (xp.reshape(n_ctx, t_ctx, d), xs.reshape(n_lat, t_lat, d),
            jnp.stack(new_k, axis=1), jnp.stack(new_v, axis=1), jnp.stack(new_dk, axis=1),
            jnp.stack(new_dv, axis=1), jnp.stack(new_s, axis=1))
```

```python
import functools
import math

import numpy as np
import jax
import jax.numpy as jnp
from jax import lax
from jax.experimental import pallas as pl
from jax.experimental.pallas import tpu as pltpu

F32 = jnp.float32
BF16 = jnp.bfloat16

HEAD_DIM = 128
GRID_W = 64
ROPE_THETA = 10000.0
NORM_EPS = 1e-6
DN_HEADS = 8
DN_CHUNK = 64
CONV_K = 3
GQA_Q_HEADS = 16
GQA_KV_HEADS = 4
GQA_GROUP = GQA_Q_HEADS // GQA_KV_HEADS
DIFF_HEADS = 4
DN_W = DN_HEADS * HEAD_DIM
GQA_W = GQA_Q_HEADS * HEAD_DIM
GQA_KV_W = GQA_KV_HEADS * HEAD_DIM
DIFF_W = DIFF_HEADS * 2 * HEAD_DIM
MIX_W = DN_W + GQA_W + DIFF_W
N_GATE_COLS = 4 * DN_HEADS

OFF_DQ, OFF_DK, OFF_DV, OFF_DG = 0, 8, 16, 24
OFF_GQ, OFF_GK, OFF_GV, OFF_GG = 32, 48, 52, 56
OFF_FQ, OFF_FK, OFF_FV, OFF_FG = 72, 80, 88, 96
Z_BLOCKS = 104
Z_COLS = Z_BLOCKS * HEAD_DIM

VMEM_LIMIT_BYTES = 56 * 1024 * 1024
MOD_ROWS = 16
LOG2E = math.log2(math.e)
PRE_UNROLL = 4


def _params(*sem):
    return pltpu.CompilerParams(dimension_semantics=sem, vmem_limit_bytes=VMEM_LIMIT_BYTES)


def _silu(x):
    return x * jax.nn.sigmoid(x)


def _bdot(a, b):
    return jnp.dot(a.astype(BF16), b.astype(BF16), preferred_element_type=F32)


def _bdot_nt(a, b):
    return lax.dot_general(a.astype(BF16), b.astype(BF16), (((1,), (1,)), ((), ())),
                           preferred_element_type=F32)


def _mod_kernel(c_ref, w_ref, b_ref, o_ref):
    o_ref[...] = _bdot(_silu(c_ref[...]), w_ref[...]) + b_ref[...]


def _modulation(cond, w_mod, b_mod):
    n_layers, d, n = w_mod.shape
    tn = min(512, n)
    return pl.pallas_call(
        _mod_kernel,
        grid=(n_layers, n // tn),
        in_specs=[
            pl.BlockSpec((MOD_ROWS, d), lambda l, j: (0, 0)),
            pl.BlockSpec((None, d, tn), lambda l, j: (l, 0, j)),
            pl.BlockSpec((None, 1, tn), lambda l, j: (l, 0, j)),
        ],
        out_specs=pl.BlockSpec((None, MOD_ROWS, tn), lambda l, j: (l, 0, j)),
        out_shape=jax.ShapeDtypeStruct((n_layers, MOD_ROWS, n), F32),
        compiler_params=_params("arbitrary", "arbitrary"),
        name="modulation",
    )(cond, w_mod, b_mod.reshape(n_layers, 1, n))


def _inproj_kernel(x_ref, sh_ref, sc_ref, pw_ref, w_ref, wab_ref, z_ref, ab_ref, h_scr):
    @pl.when(pl.program_id(1) == 0)
    def _():
        x = x_ref[...]
        y = x * lax.rsqrt(jnp.mean(x * x, axis=-1, keepdims=True) + NORM_EPS) * pw_ref[...]
        h = (y * (1.0 + sc_ref[...]) + sh_ref[...]).astype(BF16)
        h_scr[...] = h
        ab_ref[...] = jnp.dot(h, wab_ref[...], preferred_element_type=F32)

    z_ref[...] = jnp.dot(h_scr[...], w_ref[...], preferred_element_type=F32).astype(z_ref.dtype)


def _in_projection(x, mod, mod_row0, rows_per_mod, layer, pre_w, w_main, w_ab):
    m, d = x.shape
    tm = min(512, rows_per_mod)
    tn = 1024
    assert rows_per_mod % tm == 0 and m % tm == 0 and Z_COLS % tn == 0
    tiles_per_mod = rows_per_mod // tm
    base = layer * MOD_ROWS + mod_row0

    def mod_map(part):
        return lambda i, j: (base + i // tiles_per_mod, 0, part)

    return pl.pallas_call(
        _inproj_kernel,
        grid=(m // tm, Z_COLS // tn),
        in_specs=[
            pl.BlockSpec((tm, d), lambda i, j: (i, 0)),
            pl.BlockSpec((None, 1, d), mod_map(0)),
            pl.BlockSpec((None, 1, d), mod_map(1)),
            pl.BlockSpec((None, 1, d), lambda i, j: (layer, 0, 0)),
            pl.BlockSpec((None, d, tn), lambda i, j: (layer, 0, j)),
            pl.BlockSpec((None, d, 128), lambda i, j: (layer, 0, 0)),
        ],
        out_specs=[
            pl.BlockSpec((tm, tn), lambda i, j: (i, j)),
            pl.BlockSpec((tm, 128), lambda i, j: (i, 0)),
        ],
        out_shape=[jax.ShapeDtypeStruct((m, Z_COLS), BF16), jax.ShapeDtypeStruct((m, 128), F32)],
        scratch_shapes=[pltpu.VMEM((tm, d), BF16)],
        compiler_params=_params("arbitrary", "arbitrary"),
        name="in_projection",
    )(x, mod, mod, pre_w, w_main, w_ab)


def _rope_tables(t_len):
    half = HEAD_DIM // 2
    t = np.arange(t_len)
    pos = np.stack([t // GRID_W, t % GRID_W], axis=-1).astype(np.float64)
    inv_freq = ROPE_THETA ** (-np.arange(0, half, 2, dtype=np.float64) / half)
    ang = pos[..., None] * inv_freq
    ang = np.concatenate([ang, ang], axis=-1).reshape(t_len, HEAD_DIM)
    first = (np.arange(HEAD_DIM) % half) < half // 2
    cos = np.cos(ang)
    sin = np.sin(ang)
    sin_up = np.where(first, -sin, 0.0)
    sin_dn = np.where(first, 0.0, sin)
    return tuple(jnp.asarray(a, F32) for a in (cos, sin_up, sin_dn))


def _rope(x, cos, sin_up, sin_dn):
    q = HEAD_DIM // 4
    return x * cos + pltpu.roll(x, HEAD_DIM - q, 1) * sin_up + pltpu.roll(x, q, 1) * sin_dn


def _rms(x, w):
    return x * lax.rsqrt(jnp.mean(x * x, axis=-1, keepdims=True) + NORM_EPS) * w


def _keyprep_kernel(*refs, rope, n_fk):
    if rope:
        gk_ref, fk_ref, kw_ref, cos_ref, su_ref, sd_ref, gko_ref, fko_ref = refs
        tabs = (cos_ref[...], su_ref[...], sd_ref[...])
    else:
        gk_ref, kw_ref, gko_ref = refs
    kw = kw_ref[...]
    for j in range(GQA_KV_HEADS):
        sl = slice(j * HEAD_DIM, (j + 1) * HEAD_DIM)
        k = _rms(gk_ref[:, sl].astype(F32), kw)
        if rope:
            k = _rope(k, *tabs)
        gko_ref[:, sl] = k.astype(gko_ref.dtype)
    if rope:
        for j in range(n_fk):
            sl = slice(j * HEAD_DIM, (j + 1) * HEAD_DIM)
            fko_ref[:, sl] = _rope(fk_ref[:, sl].astype(F32), *tabs).astype(fko_ref.dtype)


def _key_prep(z, t_len, layer, k_norm_w, rope):
    m = z.shape[0]
    tm = min(256, t_len)
    tiles_per_seq = t_len // tm
    kw_spec = pl.BlockSpec((None, 1, HEAD_DIM), lambda i: (layer, 0, 0))
    gk_spec = pl.BlockSpec((tm, GQA_KV_W), lambda i: (i, OFF_GK * HEAD_DIM // GQA_KV_W))
    if rope:
        tabs = _rope_tables(t_len)
        tab_spec = pl.BlockSpec((tm, HEAD_DIM), lambda i: (i % tiles_per_seq, 0))
        return pl.pallas_call(
            functools.partial(_keyprep_kernel, rope=True, n_fk=2 * DIFF_HEADS),
            grid=(m // tm,),
            in_specs=[gk_spec, pl.BlockSpec((tm, DIFF_W), lambda i: (i, OFF_FK * HEAD_DIM // DIFF_W)),
                      kw_spec, tab_spec, tab_spec, tab_spec],
            out_specs=[pl.BlockSpec((tm, GQA_KV_W), lambda i: (i, 0)),
                       pl.BlockSpec((tm, DIFF_W), lambda i: (i, 0))],
            out_shape=[jax.ShapeDtypeStruct((m, GQA_KV_W), BF16), jax.ShapeDtypeStruct((m, DIFF_W), BF16)],
            compiler_params=_params("arbitrary"),
            name="key_prep_rope",
        )(z, z, k_norm_w, *tabs)
    return pl.pallas_call(
        functools.partial(_keyprep_kernel, rope=False, n_fk=0),
        grid=(m // tm,),
        in_specs=[gk_spec, kw_spec],
        out_specs=pl.BlockSpec((tm, GQA_KV_W), lambda i: (i, 0)),
        out_shape=jax.ShapeDtypeStruct((m, GQA_KV_W), F32),
        compiler_params=_params("arbitrary"),
        name="key_prep",
    )(z, k_norm_w)


def _fill_vt(v_ref, vt_ref, tk):
    def body(c, carry):
        r0 = pl.multiple_of(c * tk, tk)
        vt_ref[c] = jnp.transpose(v_ref[pl.ds(r0, tk), :].astype(F32)).astype(BF16)
        return carry

    lax.fori_loop(0, v_ref.shape[0] // tk, body, 0)


def _attend_t(q, k_ref, vt_ref, ck_ref, cvt_ref, s_refs, m_ref, l_ref, acc_ref, tk):
    m_ref[...] = jnp.full(m_ref.shape, -jnp.inf, F32)
    l_ref[...] = jnp.zeros(l_ref.shape, F32)
    acc_ref[...] = jnp.zeros(acc_ref.shape, F32)
    s_a, s_b = s_refs
    n_chunks = k_ref.shape[0] // tk

    def scores(k, s_ref):
        s_ref[0:k.shape[0], :] = _bdot_nt(k, q)

    def softmax_pv(s_ref, vt):
        n = vt.shape[1]
        m_prev = m_ref[...]
        m_new = jnp.maximum(m_prev, jnp.max(s_ref[0:n, :], axis=0, keepdims=True))
        alpha = jnp.exp2(m_prev - m_new)
        p = jnp.exp2(s_ref[0:n, :] - m_new)
        l_ref[...] = alpha * l_ref[...] + jnp.sum(p, axis=0, keepdims=True)
        acc_ref[...] = acc_ref[...] * alpha + jnp.dot(vt, p.astype(BF16), preferred_element_type=F32)
        m_ref[...] = m_new

    def k_chunk(c):
        return k_ref[pl.ds(pl.multiple_of(c * tk, tk), tk), :]

    scores(k_chunk(0), s_a)
    if n_chunks > 1:
        assert n_chunks % 2 == 0

        def body(i, carry):
            scores(k_chunk(2 * i + 1), s_b)
            softmax_pv(s_a, vt_ref[2 * i])
            scores(k_chunk(2 * i + 2), s_a)
            softmax_pv(s_b, vt_ref[2 * i + 1])
            return carry

        lax.fori_loop(0, n_chunks // 2 - 1, body, 0)
        scores(k_chunk(n_chunks - 1), s_b)
        softmax_pv(s_a, vt_ref[n_chunks - 2])
        if ck_ref is not None:
            scores(ck_ref[...], s_a)
        softmax_pv(s_b, vt_ref[n_chunks - 1])
        if ck_ref is not None:
            softmax_pv(s_a, cvt_ref[...])
    else:
        softmax_pv(s_a, vt_ref[0])
        if ck_ref is not None:
            scores(ck_ref[...], s_b)
            softmax_pv(s_b, cvt_ref[...])
    return acc_ref[...] / l_ref[...]


def _gqa_kernel(*refs, rope, cached, tq, tk):
    refs = list(refs)
    q_ref, g_ref, k_ref, v_ref, qw_ref = refs[:5]
    pos = 5
    ck_ref = cv_ref = cvt_s = None
    if cached:
        ck_ref, cv_ref = refs[pos:pos + 2]
        pos += 2
    if rope:
        tabs = tuple(r[...] for r in refs[pos:pos + 3])
        pos += 3
    o_ref, q_scr, vt_s = refs[pos:pos + 3]
    pos += 3
    if cached:
        cvt_s = refs[pos]
        pos += 1
    s_a, s_b, m_ref, l_ref, acc_ref = refs[pos:]

    @pl.when(pl.program_id(2) == 0)
    def _():
        _fill_vt(v_ref, vt_s, tk)
        if cached:
            cvt_s[...] = jnp.transpose(cv_ref[...]).astype(BF16)

    qw = qw_ref[...]
    scale = HEAD_DIM ** -0.5 * LOG2E
    for j in range(GQA_GROUP):
        qj = _rms(q_ref[:, j * HEAD_DIM:(j + 1) * HEAD_DIM].astype(F32), qw)
        if rope:
            qj = _rope(qj, *tabs)
        q_scr[j * tq:(j + 1) * tq, :] = (qj * scale).astype(BF16)
    o = jnp.transpose(_attend_t(q_scr[...], k_ref, vt_s, ck_ref, cvt_s, (s_a, s_b), m_ref, l_ref, acc_ref, tk))
    for j in range(GQA_GROUP):
        sl = slice(j * HEAD_DIM, (j + 1) * HEAD_DIM)
        o_ref[:, sl] = (o[j * tq:(j + 1) * tq, :] * _silu(g_ref[:, sl].astype(F32))).astype(o_ref.dtype)


def _gqa_attention(z, keys, t_len, layer, q_norm_w, cache_k, cache_v, rope):
    m = z.shape[0]
    n_seq = m // t_len
    tq = min(256, t_len)
    tk = min(512, t_len)
    nq = t_len // tq
    cached = cache_k is not None
    qcol = OFF_GQ * HEAD_DIM // (GQA_GROUP * HEAD_DIM)
    gcol = OFF_GG * HEAD_DIM // (GQA_GROUP * HEAD_DIM)
    in_specs = [
        pl.BlockSpec((tq, GQA_GROUP * HEAD_DIM), lambda b, g, i: (b * nq + i, qcol + g)),
        pl.BlockSpec((tq, GQA_GROUP * HEAD_DIM), lambda b, g, i: (b * nq + i, gcol + g)),
        pl.BlockSpec((t_len, HEAD_DIM), lambda b, g, i: (b, g)),
        pl.BlockSpec((t_len, HEAD_DIM), lambda b, g, i: (b, OFF_GV + g)),
        pl.BlockSpec((None, 1, HEAD_DIM), lambda b, g, i: (layer, 0, 0)),
    ]
    args = [z, z, keys, z, q_norm_w]
    rows = GQA_GROUP * tq
    scratch = [pltpu.VMEM((rows, HEAD_DIM), BF16), pltpu.VMEM((t_len // tk, HEAD_DIM, tk), BF16)]
    if cached:
        p_len = cache_k.shape[2]
        spec = pl.BlockSpec((None, None, p_len, HEAD_DIM), lambda b, g, i: (b, layer, 0, g))
        in_specs += [spec, spec]
        args += [cache_k, cache_v]
        scratch.append(pltpu.VMEM((HEAD_DIM, p_len), BF16))
    if rope:
        spec = pl.BlockSpec((tq, HEAD_DIM), lambda b, g, i: (i, 0))
        in_specs += [spec, spec, spec]
        args += list(_rope_tables(t_len))
    scratch += [pltpu.VMEM((tk, rows), F32), pltpu.VMEM((tk, rows), F32),
                pltpu.VMEM((1, rows), F32), pltpu.VMEM((1, rows), F32), pltpu.VMEM((HEAD_DIM, rows), F32)]
    return pl.pallas_call(
        functools.partial(_gqa_kernel, rope=rope, cached=cached, tq=tq, tk=tk),
        grid=(n_seq, GQA_KV_HEADS, nq),
        in_specs=in_specs,
        out_specs=pl.BlockSpec((tq, GQA_GROUP * HEAD_DIM), lambda b, g, i: (b * nq + i, g)),
        out_shape=jax.ShapeDtypeStruct((m, GQA_W), BF16),
        scratch_shapes=scratch,
        compiler_params=_params("arbitrary", "arbitrary", "arbitrary"),
        name="gqa_attention",
    )(*args)


def _diff_kernel(*refs, rope, cached, tk, lambda_init):
    refs = list(refs)
    q_ref, g_ref, k_ref, v_ref, lam_ref, nw_ref = refs[:6]
    pos = 6
    ck_ref = cv_ref = cvt_s = None
    if cached:
        ck_ref, cv_ref = refs[pos:pos + 2]
        pos += 2
    if rope:
        tabs = tuple(r[...] for r in refs[pos:pos + 3])
        pos += 3
    o_ref, vt_s = refs[pos:pos + 2]
    pos += 2
    if cached:
        cvt_s = refs[pos]
        pos += 1
    o1_s, s_a, s_b, m_ref, l_ref, acc_ref = refs[pos:]

    @pl.when(pl.program_id(2) == 0)
    def _():
        _fill_vt(v_ref, vt_s, tk)
        if cached:
            cvt_s[...] = jnp.transpose(cv_ref[...]).astype(BF16)

    scale = HEAD_DIM ** -0.5 * LOG2E
    lp = lam_ref[...]
    lam = (jnp.exp(jnp.sum(lp[0:1] * lp[1:2], axis=-1, keepdims=True))
           - jnp.exp(jnp.sum(lp[2:3] * lp[3:4], axis=-1, keepdims=True)) + lambda_init)
    for i in range(2):
        sl = slice(i * HEAD_DIM, (i + 1) * HEAD_DIM)
        qi = q_ref[:, sl].astype(F32)
        if rope:
            qi = _rope(qi, *tabs)
        qi = (qi * scale).astype(BF16)
        o_t = _attend_t(qi, k_ref.at[:, sl], vt_s, None if ck_ref is None else ck_ref.at[:, sl],
                        cvt_s, (s_a, s_b), m_ref, l_ref, acc_ref, tk)
        if i == 0:
            o1_s[...] = o_t
    o = jnp.transpose(o1_s[...] - lam * o_t)
    o = _rms(o, nw_ref[...]) * (1.0 - lambda_init)
    o_ref[...] = (o * _silu(g_ref[...].astype(F32))).astype(o_ref.dtype)


def _diff_attention(z, keys, key_col0, t_len, layer, lambda_init, diff_lambda, diff_norm_w,
                    cache_k, cache_v, rope):
    m = z.shape[0]
    n_seq = m // t_len
    tq = min(512, t_len)
    tk = min(512, t_len)
    nq = t_len // tq
    hw = 2 * HEAD_DIM
    cached = cache_k is not None
    in_specs = [
        pl.BlockSpec((tq, hw), lambda b, h, i: (b * nq + i, OFF_FQ // 2 + h)),
        pl.BlockSpec((tq, hw), lambda b, h, i: (b * nq + i, OFF_FG // 2 + h)),
        pl.BlockSpec((t_len, hw), lambda b, h, i: (b, key_col0 + h)),
        pl.BlockSpec((t_len, hw), lambda b, h, i: (b, OFF_FV // 2 + h)),
        pl.BlockSpec((None, 4, HEAD_DIM), lambda b, h, i: (layer, 0, 0)),
        pl.BlockSpec((None, 1, hw), lambda b, h, i: (layer, 0, 0)),
    ]
    args = [z, z, keys, z, diff_lambda, diff_norm_w]
    scratch = [pltpu.VMEM((t_len // tk, hw, tk), BF16)]
    if cached:
        p_len = cache_k.shape[2]
        spec = pl.BlockSpec((None, None, p_len, hw), lambda b, h, i: (b, layer, 0, h))
        in_specs += [spec, spec]
        args += [cache_k, cache_v]
        scratch.append(pltpu.VMEM((hw, p_len), BF16))
    if rope:
        spec = pl.BlockSpec((tq, HEAD_DIM), lambda b, h, i: (i, 0))
        in_specs += [spec, spec, spec]
        args += list(_rope_tables(t_len))
    scratch += [pltpu.VMEM((hw, tq), F32), pltpu.VMEM((tk, tq), F32), pltpu.VMEM((tk, tq), F32),
                pltpu.VMEM((1, tq), F32), pltpu.VMEM((1, tq), F32), pltpu.VMEM((hw, tq), F32)]
    return pl.pallas_call(
        functools.partial(_diff_kernel, rope=rope, cached=cached, tk=tk, lambda_init=lambda_init),
        grid=(n_seq, DIFF_HEADS, nq),
        in_specs=in_specs,
        out_specs=pl.BlockSpec((tq, hw), lambda b, h, i: (b * nq + i, h)),
        out_shape=jax.ShapeDtypeStruct((m, DIFF_W), BF16),
        scratch_shapes=scratch,
        compiler_params=_params("arbitrary", "arbitrary", "arbitrary"),
        name="diff_attention",
    )(*args)


def _split_bf16(a):
    hi = a.astype(BF16)
    return hi, (a - hi.astype(F32)).astype(BF16)


def _mm3(a_parts, b_parts):
    a_hi, a_lo = a_parts
    b_hi, b_lo = b_parts
    return jnp.dot(jnp.concatenate([a_hi, a_hi, a_lo], axis=1), jnp.concatenate([b_hi, b_lo, b_hi], axis=0),
                   preferred_element_type=F32)


def _unit_tri_inverses(l_mats):
    n = l_mats[0].shape[0]
    eye = (lax.broadcasted_iota(jnp.int32, (n, n), 0) == lax.broadcasted_iota(jnp.int32, (n, n), 1)).astype(F32)
    ts = [eye - l for l in l_mats]
    l_parts = [_split_bf16(l) for l in l_mats]
    ps = [_mm3(lp, lp) for lp in l_parts]
    steps = int(math.log2(DN_CHUNK)) - 1
    for s in range(steps):
        p_parts = [_split_bf16(p) for p in ps]
        ts = [t + _mm3(_split_bf16(t), pp) for t, pp in zip(ts, p_parts)]
        if s + 1 < steps:
            ps = [_mm3(pp, pp) for pp in p_parts]
    return ts


def _chunk_cumsum(x, reverse):
    n = x.shape[0]
    row = lax.broadcasted_iota(jnp.int32, x.shape, 0)
    s = 1
    while s < n:
        if reverse:
            x = x + jnp.where(row < n - s, pltpu.roll(x, n - s, 0), 0.0)
        else:
            x = x + jnp.where(row >= s, pltpu.roll(x, s, 0), 0.0)
        s *= 2
    return x


def _dn_kernel(*refs, t_len, has_state):
    refs = list(refs)
    (zq_ref, zk_ref, zv_ref, zg_ref, ab_ref, cwq_ref, cwk_ref, cwv_ref,
     alog_ref, dtb_ref, nw_ref) = refs[:11]
    pos = 11
    s0_ref = None
    if has_state:
        s0_ref = refs[pos]
        pos += 1
    o_ref, sfin_ref = refs[pos:pos + 2]
    (xq_s, xk_s, xv_s, u_s, w_s, qd_s, kdt_s, qk_s, et_s, s_s, o_s, rhs_s) = refs[pos + 2:]
    c_len = DN_CHUNK
    n_chunks = t_len // c_len
    pad = 8
    head = pl.program_id(1)

    zero_pad = jnp.zeros((pad, HEAD_DIM), F32)
    for dst in (xq_s, xk_s, xv_s):
        dst[0:pad, :] = zero_pad
        dst[pad + t_len:2 * pad + t_len, :] = zero_pad

    def fill(c, carry):
        r0 = pl.multiple_of(c * c_len, c_len)
        for src, dst in ((zq_ref, xq_s), (zk_ref, xk_s), (zv_ref, xv_s)):
            dst[pl.ds(r0 + pad, c_len), :] = src[pl.ds(r0, c_len), :].astype(F32)
        return carry

    lax.fori_loop(0, n_chunks, fill, 0)

    def conv(x_s, cw_ref, r0):
        n = c_len + 2 * pad
        win = x_s[pl.ds(r0, n), :]
        prev = pltpu.roll(win, 1, 0)[pad:pad + c_len]
        nxt = pltpu.roll(win, n - 1, 0)[pad:pad + c_len]
        cw = cw_ref[...]
        return _silu(prev * cw[0:1] + win[pad:pad + c_len] * cw[1:2] + nxt * cw[2:3])

    def l2n(x):
        return x * lax.rsqrt(jnp.sum(x * x, axis=-1, keepdims=True) + NORM_EPS)

    lane = lax.broadcasted_iota(jnp.int32, (c_len, 128), 1)
    row2 = lax.broadcasted_iota(jnp.int32, (2 * c_len, 2 * c_len), 0)
    col2 = lax.broadcasted_iota(jnp.int32, (2 * c_len, 2 * c_len), 1)
    fwd_blk = (row2 < c_len) & (col2 < c_len)
    bwd_blk = (row2 >= c_len) & (col2 >= c_len)
    incl = (fwd_blk & (row2 >= col2)) | (bwd_blk & (row2 <= col2))
    strict = (fwd_blk & (row2 > col2)) | (bwd_blk & (row2 < col2))
    neg_a = -jnp.exp(alog_ref[...])
    dtb = dtb_ref[...]

    def pick(x, col):
        return jnp.broadcast_to(jnp.sum(jnp.where(lane == col, x, 0.0), axis=-1, keepdims=True),
                                (c_len, HEAD_DIM))

    def both(x):
        return jnp.concatenate([x, x], axis=0)

    def precompute(c, j):
        r0 = pl.multiple_of(c * c_len, c_len)
        q2 = both(l2n(conv(xq_s, cwq_ref, r0)) * (HEAD_DIM ** -0.5))
        k2 = both(l2n(conv(xk_s, cwk_ref, r0)))
        v2 = both(conv(xv_s, cwv_ref, r0))
        ab = ab_ref[pl.ds(r0, c_len), :]
        x = ab + dtb
        g_all = neg_a * (jnp.maximum(x, 0.0) + jnp.log1p(jnp.exp(-jnp.abs(x))))
        b_all = jax.nn.sigmoid(ab)
        g_f = pick(g_all, head)
        g_b = pick(g_all, DN_HEADS + head)
        beta2 = jnp.concatenate([pick(b_all, 2 * DN_HEADS + head), pick(b_all, 3 * DN_HEADS + head)], axis=0)
        gc2 = jnp.concatenate([_chunk_cumsum(g_f, False), _chunk_cumsum(g_b, True)], axis=0)
        tot_f = jnp.sum(g_f, axis=0, keepdims=True)
        tot_b = jnp.sum(g_b, axis=0, keepdims=True)
        tot2 = jnp.concatenate([jnp.broadcast_to(tot_f, (c_len, HEAD_DIM)),
                                jnp.broadcast_to(tot_b, (c_len, HEAD_DIM))], axis=0)
        k2b = k2.astype(BF16)
        kk2 = _bdot_nt(k2b, k2b)
        qk2 = _bdot_nt(q2, k2b)
        decay = jnp.where(incl, jnp.exp(jnp.where(incl, gc2 - jnp.transpose(gc2), 0.0)), 0.0)
        e2 = jnp.exp(gc2)
        rhs_s[j] = jnp.concatenate([v2 * beta2, k2 * beta2 * e2], axis=-1).astype(BF16)
        qkm = (qk2 * decay).astype(BF16)
        qd2 = (q2 * e2).astype(BF16)
        kdt_s[c] = jnp.transpose(k2 * jnp.exp(tot2 - gc2)).astype(BF16)
        for d in range(2):
            rows = slice(d * c_len, (d + 1) * c_len)
            qd_s[d, pl.ds(r0, c_len), :] = qd2[rows]
            qk_s[d, pl.ds(r0, c_len), :] = qkm[rows]
        et_s[0, c] = jnp.broadcast_to(jnp.exp(tot_f), (8, HEAD_DIM))
        et_s[1, c] = jnp.broadcast_to(jnp.exp(tot_b), (8, HEAD_DIM))
        return jnp.where(strict, kk2 * beta2 * decay, 0.0)

    def precompute_group(i, carry):
        c0 = PRE_UNROLL * i
        l_mats = [precompute(c0 + j, j) for j in range(PRE_UNROLL)]
        t_invs = _unit_tri_inverses(l_mats)
        for j in range(PRE_UNROLL):
            r0 = pl.multiple_of((c0 + j) * c_len, c_len)
            uw = jnp.dot(t_invs[j].astype(BF16), rhs_s[j], preferred_element_type=F32)
            for d in range(2):
                rows = slice(d * c_len, (d + 1) * c_len)
                u_s[d, pl.ds(r0, c_len), :] = uw[rows, 0:HEAD_DIM]
                w_s[d, pl.ds(r0, c_len), :] = uw[rows, HEAD_DIM:].astype(BF16)
        return carry

    lax.fori_loop(0, n_chunks // PRE_UNROLL, precompute_group, 0)

    if has_state:
        s_s[...] = s0_ref[...].astype(F32)
    else:
        s_s[...] = jnp.zeros(s_s.shape, F32)
    zeros_c = jnp.zeros((c_len, HEAD_DIM), BF16)

    def scan(i, carry):
        cs = (i, n_chunks - 1 - i)
        r0s = [pl.multiple_of(c * c_len, c_len) for c in cs]
        ss = [s_s[d] for d in range(2)]
        wqs = [_bdot(jnp.concatenate([w_s[d, pl.ds(r0s[d], c_len), :], qd_s[d, pl.ds(r0s[d], c_len), :]], axis=0),
                     ss[d]) for d in range(2)]
        vbs = [(u_s[d, pl.ds(r0s[d], c_len), :] - wqs[d][0:c_len]).astype(BF16) for d in range(2)]
        v_pads = [jnp.concatenate([vbs[0], zeros_c], axis=0), jnp.concatenate([zeros_c, vbs[1]], axis=0)]
        for d in range(2):
            o_s[d, pl.ds(r0s[d], c_len), :] = wqs[d][c_len:] + _bdot(qk_s[d, pl.ds(r0s[d], c_len), :], v_pads[d])
        for d in range(2):
            s_s[d] = ss[d] * et_s[d, cs[d]][0:1, :] + _bdot(kdt_s[cs[d]], v_pads[d])
        return carry

    lax.fori_loop(0, n_chunks, scan, 0)
    sfin_ref[...] = s_s[...].astype(sfin_ref.dtype)

    nw = nw_ref[...]

    def finish(c, carry):
        r0 = pl.multiple_of(c * c_len, c_len)
        o = o_s[0, pl.ds(r0, c_len), :] + o_s[1, pl.ds(r0, c_len), :]
        o = _rms(o, nw) * _silu(zg_ref[pl.ds(r0, c_len), :].astype(F32))
        o_ref[pl.ds(r0, c_len), :] = o.astype(o_ref.dtype)
        return carry

    lax.fori_loop(0, n_chunks, finish, 0)


def _delta_net(z, ab, t_len, layer, conv_w, a_log_row, dt_bias_row, dn_norm_w, state):
    m = z.shape[0]
    n_seq = m // t_len
    n_chunks = t_len // DN_CHUNK
    assert n_chunks % PRE_UNROLL == 0
    has_state = state is not None

    def zspec(off):
        return pl.BlockSpec((t_len, HEAD_DIM), lambda b, h: (b, off + h))

    def cwspec(off):
        return pl.BlockSpec((None, CONV_K, HEAD_DIM), lambda b, h: (layer, 0, off + h))

    row_spec = pl.BlockSpec((None, 1, 128), lambda b, h: (layer, 0, 0))
    in_specs = [zspec(OFF_DQ), zspec(OFF_DK), zspec(OFF_DV), zspec(OFF_DG),
                pl.BlockSpec((t_len, 128), lambda b, h: (b, 0)),
                cwspec(0), cwspec(DN_HEADS), cwspec(2 * DN_HEADS),
                row_spec, row_spec, row_spec]
    args = [z, z, z, z, ab, conv_w, conv_w, conv_w, a_log_row, dt_bias_row, dn_norm_w]
    if has_state:
        in_specs.append(pl.BlockSpec((None, None, 2, None, HEAD_DIM, HEAD_DIM),
                                     lambda b, h: (b, layer, 0, h, 0, 0)))
        args.append(state)
    tt = t_len
    scratch = [pltpu.VMEM((tt + 16, HEAD_DIM), F32)] * 3 + [
        pltpu.VMEM((2, tt, HEAD_DIM), F32),
        pltpu.VMEM((2, tt, HEAD_DIM), BF16),
        pltpu.VMEM((2, tt, HEAD_DIM), BF16),
        pltpu.VMEM((n_chunks, HEAD_DIM, 2 * DN_CHUNK), BF16),
        pltpu.VMEM((2, tt, 2 * DN_CHUNK), BF16),
        pltpu.VMEM((2, n_chunks, 8, HEAD_DIM), F32),
        pltpu.VMEM((2, HEAD_DIM, HEAD_DIM), F32),
        pltpu.VMEM((2, tt, HEAD_DIM), F32),
        pltpu.VMEM((PRE_UNROLL, 2 * DN_CHUNK, 2 * HEAD_DIM), BF16),
    ]
    return pl.pallas_call(
        functools.partial(_dn_kernel, t_len=t_len, has_state=has_state),
        grid=(n_seq, DN_HEADS),
        in_specs=in_specs,
        out_specs=[pl.BlockSpec((t_len, HEAD_DIM), lambda b, h: (b, h)),
                   pl.BlockSpec((None, 2, None, HEAD_DIM, HEAD_DIM), lambda b, h: (b, 0, h, 0, 0))],
        out_shape=[jax.ShapeDtypeStruct((m, DN_W), BF16),
                   jax.ShapeDtypeStruct((n_seq, 2, DN_HEADS, HEAD_DIM, HEAD_DIM), F32)],
        scratch_shapes=scratch,
        compiler_params=_params("arbitrary", "arbitrary"),
        name="delta_net",
    )(*args)


def _outproj_kernel(a1_ref, a2_ref, a3_ref, w_ref, o_ref):
    acc = jnp.dot(a1_ref[...], w_ref[0:DN_W, :], preferred_element_type=F32)
    acc += jnp.dot(a2_ref[...], w_ref[DN_W:DN_W + GQA_W, :], preferred_element_type=F32)
    acc += jnp.dot(a3_ref[...], w_ref[DN_W + GQA_W:MIX_W, :], preferred_element_type=F32)
    o_ref[...] = acc.astype(o_ref.dtype)


def _out_projection(dn, gqa, diff, w_out, layer):
    m = dn.shape[0]
    d = w_out.shape[-1]
    tm = min(1024, m)
    tn = min(1024, d)
    return pl.pallas_call(
        _outproj_kernel,
        grid=(m // tm, d // tn),
        in_specs=[pl.BlockSpec((tm, DN_W), lambda i, j: (i, 0)),
                  pl.BlockSpec((tm, GQA_W), lambda i, j: (i, 0)),
                  pl.BlockSpec((tm, DIFF_W), lambda i, j: (i, 0)),
                  pl.BlockSpec((None, MIX_W, tn), lambda i, j: (layer, 0, j))],
        out_specs=pl.BlockSpec((tm, tn), lambda i, j: (i, j)),
        out_shape=jax.ShapeDtypeStruct((m, d), BF16),
        compiler_params=_params("arbitrary", "arbitrary"),
        name="out_projection",
    )(dn, gqa, diff, w_out)


def _residual_kernel(x_ref, mix_ref, g_ref, pw_ref, o_ref):
    o_ref[...] = x_ref[...] + g_ref[...] * _rms(mix_ref[...].astype(F32), pw_ref[...])


def _post_residual(x, mix, mod, mod_row0, rows_per_mod, layer, post_w):
    m, d = x.shape
    tm = min(256, rows_per_mod)
    tiles_per_mod = rows_per_mod // tm
    base = layer * MOD_ROWS + mod_row0
    return pl.pallas_call(
        _residual_kernel,
        grid=(m // tm,),
        in_specs=[pl.BlockSpec((tm, d), lambda i: (i, 0)),
                  pl.BlockSpec((tm, d), lambda i: (i, 0)),
                  pl.BlockSpec((None, 1, d), lambda i: (base + i // tiles_per_mod, 0, 2)),
                  pl.BlockSpec((None, 1, d), lambda i: (layer, 0, 0))],
        out_specs=pl.BlockSpec((tm, d), lambda i: (i, 0)),
        out_shape=jax.ShapeDtypeStruct((m, d), F32),
        compiler_params=_params("arbitrary"),
        name="post_residual",
    )(x, mix, mod, post_w)


def _mixer_layer(x, t_len, mod, mod_row0, rows_per_mod, layer, lambda_init, wts, caches):
    (pre_w, post_w, w_main, w_ab, w_out, conv_w, a_log_row, dt_bias_row, dn_norm_w,
     q_norm_w, k_norm_w, diff_lambda, diff_norm_w) = wts
    latent = caches is not None
    z, ab = _in_projection(x, mod, mod_row0, rows_per_mod, layer, pre_w, w_main, w_ab)
    if latent:
        cgk, cgv, cfk, cfv, state = caches
        gkeys, fkeys = _key_prep(z, t_len, layer, k_norm_w, rope=True)
        fkey_col0 = 0
    else:
        cgk = cgv = cfk = cfv = state = None
        gkeys = _key_prep(z, t_len, layer, k_norm_w, rope=False)
        fkeys, fkey_col0 = z, OFF_FK // 2
    dn, s_fin = _delta_net(z, ab, t_len, layer, conv_w, a_log_row, dt_bias_row, dn_norm_w, state)
    gqa = _gqa_attention(z, gkeys, t_len, layer, q_norm_w, cgk, cgv, rope=latent)
    diff = _diff_attention(z, fkeys, fkey_col0, t_len, layer, lambda_init, diff_lambda, diff_norm_w,
                           cfk, cfv, rope=latent)
    mix = _out_projection(dn, gqa, diff, w_out, layer)
    y = _post_residual(x, mix, mod, mod_row0, rows_per_mod, layer, post_w)
    return y, z, gkeys, s_fin


def kernel(x_prompt, x_sample, cache_gqa_k, cache_gqa_v, cache_diff_k, cache_diff_v, state_dn, c, c_ctx,
           w_mod, b_mod, pre_norm_w, post_norm_w, w_in, w_out, dn_conv_w, dn_a_log, dn_dt_bias, dn_norm_w,
           gqa_q_norm_w, gqa_k_norm_w, diff_lambda, diff_norm_w):
    n_ctx, t_ctx, d = x_prompt.shape
    n_lat, t_lat, _ = x_sample.shape
    depth = w_in.shape[0]
    p_len = cache_gqa_k.shape[2]
    assert n_lat + 1 <= MOD_ROWS

    n_dn = 4 * DN_W
    w_main = jnp.concatenate([w_in[:, :, :n_dn], w_in[:, :, n_dn + N_GATE_COLS:]], axis=-1).astype(BF16)
    w_ab = jnp.pad(w_in[:, :, n_dn:n_dn + N_GATE_COLS], ((0, 0), (0, 0), (0, 128 - N_GATE_COLS))).astype(BF16)
    w_out_b = w_out.astype(BF16)
    row128 = lambda a: jnp.pad(a.reshape(depth, 1, -1), ((0, 0), (0, 0), (0, 128 - a[0].size)))
    a_log_row = row128(dn_a_log)
    dt_bias_row = row128(dn_dt_bias)
    r3 = lambda a: a.reshape(depth, 1, a.shape[-1])

    cond = jnp.concatenate([c_ctx[None, :], c, jnp.zeros((MOD_ROWS - 1 - n_lat, d), F32)], axis=0)
    mod = _modulation(cond, w_mod, b_mod).reshape(depth * MOD_ROWS, 1, 3 * d)

    cgk = cache_gqa_k.reshape(n_lat, depth, p_len, GQA_KV_W)
    cgv = cache_gqa_v.reshape(n_lat, depth, p_len, GQA_KV_W)
    cfk = cache_diff_k.reshape(n_lat, depth, p_len, DIFF_W)
    cfv = cache_diff_v.reshape(n_lat, depth, p_len, DIFF_W)

    xp = x_prompt.reshape(n_ctx * t_ctx, d)
    xs = x_sample.reshape(n_lat * t_lat, d)
    new_k, new_v, new_dk, new_dv, new_s = [], [], [], [], []
    for l in range(depth):
        lambda_init = 0.8 - 0.6 * math.exp(-0.3 * l)
        wts = (r3(pre_norm_w), r3(post_norm_w), w_main, w_ab, w_out_b, dn_conv_w, a_log_row, dt_bias_row,
               r3(dn_norm_w), r3(gqa_q_norm_w), r3(gqa_k_norm_w), diff_lambda, r3(diff_norm_w))
        xp, z, gk, s_fin = _mixer_layer(xp, t_ctx, mod, 0, n_ctx * t_ctx, l, lambda_init, wts, None)
        zc = lambda off, n: z[:, off * HEAD_DIM:off * HEAD_DIM + n].astype(F32)
        new_k.append(gk.reshape(n_ctx, t_ctx, GQA_KV_HEADS, HEAD_DIM))
        new_v.append(zc(OFF_GV, GQA_KV_W).reshape(n_ctx, t_ctx, GQA_KV_HEADS, HEAD_DIM))
        new_dk.append(zc(OFF_FK, DIFF_W).reshape(n_ctx, t_ctx, DIFF_HEADS, 2, HEAD_DIM))
        new_dv.append(zc(OFF_FV, DIFF_W).reshape(n_ctx, t_ctx, DIFF_HEADS, 2 * HEAD_DIM))
        new_s.append(s_fin)
        xs, _, _, _ = _mixer_layer(xs, t_lat, mod, 1, t_lat, l, lambda_init, wts,
                                   (cgk, cgv, cfk, cfv, state_dn))
    return (xp.reshape(n_ctx, t_ctx, d), xs.reshape(n_lat, t_lat, d),
            jnp.stack(new_k, axis=1), jnp.stack(new_v, axis=1), jnp.stack(new_dk, axis=1),
            jnp.stack(new_dv, axis=1), jnp.stack(new_s, axis=1))
```

```python
import functools
import math

import numpy as np
import jax
import jax.numpy as jnp
from jax import lax
from jax.experimental import pallas as pl
from jax.experimental.pallas import tpu as pltpu

F32 = jnp.float32
BF16 = jnp.bfloat16

HEAD_DIM = 128
GRID_W = 64
ROPE_THETA = 10000.0
NORM_EPS = 1e-6
DN_HEADS = 8
DN_CHUNK = 64
CONV_K = 3
GQA_Q_HEADS = 16
GQA_KV_HEADS = 4
GQA_GROUP = GQA_Q_HEADS // GQA_KV_HEADS
DIFF_HEADS = 4
DN_W = DN_HEADS * HEAD_DIM
GQA_W = GQA_Q_HEADS * HEAD_DIM
GQA_KV_W = GQA_KV_HEADS * HEAD_DIM
DIFF_W = DIFF_HEADS * 2 * HEAD_DIM
MIX_W = DN_W + GQA_W + DIFF_W
N_GATE_COLS = 4 * DN_HEADS

OFF_DQ, OFF_DK, OFF_DV, OFF_DG = 0, 8, 16, 24
OFF_GQ, OFF_GK, OFF_GV, OFF_GG = 32, 48, 52, 56
OFF_FQ, OFF_FK, OFF_FV, OFF_FG = 72, 80, 88, 96
Z_BLOCKS = 104
Z_COLS = Z_BLOCKS * HEAD_DIM

VMEM_LIMIT_BYTES = 56 * 1024 * 1024
MOD_ROWS = 16
LOG2E = math.log2(math.e)
PRE_UNROLL = 4
DN_BLOCK = 256


def _params(*sem):
    return pltpu.CompilerParams(dimension_semantics=sem, vmem_limit_bytes=VMEM_LIMIT_BYTES)


def _silu(x):
    return x * jax.nn.sigmoid(x)


def _bdot(a, b):
    return jnp.dot(a.astype(BF16), b.astype(BF16), preferred_element_type=F32)


def _bdot_nt(a, b):
    return lax.dot_general(a.astype(BF16), b.astype(BF16), (((1,), (1,)), ((), ())),
                           preferred_element_type=F32)


def _mod_kernel(c_ref, w_ref, b_ref, o_ref):
    o_ref[...] = _bdot(_silu(c_ref[...]), w_ref[...]) + b_ref[...]


def _modulation(cond, w_mod, b_mod):
    n_layers, d, n = w_mod.shape
    tn = min(512, n)
    return pl.pallas_call(
        _mod_kernel,
        grid=(n_layers, n // tn),
        in_specs=[
            pl.BlockSpec((MOD_ROWS, d), lambda l, j: (0, 0)),
            pl.BlockSpec((None, d, tn), lambda l, j: (l, 0, j)),
            pl.BlockSpec((None, 1, tn), lambda l, j: (l, 0, j)),
        ],
        out_specs=pl.BlockSpec((None, MOD_ROWS, tn), lambda l, j: (l, 0, j)),
        out_shape=jax.ShapeDtypeStruct((n_layers, MOD_ROWS, n), F32),
        compiler_params=_params("arbitrary", "arbitrary"),
        name="modulation",
    )(cond, w_mod, b_mod.reshape(n_layers, 1, n))


def _inproj_kernel(x_ref, sh_ref, sc_ref, pw_ref, w_ref, wab_ref, z_ref, ab_ref, h_scr):
    @pl.when(pl.program_id(1) == 0)
    def _():
        x = x_ref[...]
        y = x * lax.rsqrt(jnp.mean(x * x, axis=-1, keepdims=True) + NORM_EPS) * pw_ref[...]
        h = (y * (1.0 + sc_ref[...]) + sh_ref[...]).astype(BF16)
        h_scr[...] = h
        ab_ref[...] = jnp.dot(h, wab_ref[...], preferred_element_type=F32)

    z_ref[...] = jnp.dot(h_scr[...], w_ref[...], preferred_element_type=F32).astype(z_ref.dtype)


def _in_projection(x, mod, mod_row0, rows_per_mod, layer, pre_w, w_main, w_ab):
    m, d = x.shape
    tm = min(512, rows_per_mod)
    tn = 1024
    assert rows_per_mod % tm == 0 and m % tm == 0 and Z_COLS % tn == 0
    tiles_per_mod = rows_per_mod // tm
    base = layer * MOD_ROWS + mod_row0

    def mod_map(part):
        return lambda i, j: (base + i // tiles_per_mod, 0, part)

    return pl.pallas_call(
        _inproj_kernel,
        grid=(m // tm, Z_COLS // tn),
        in_specs=[
            pl.BlockSpec((tm, d), lambda i, j: (i, 0)),
            pl.BlockSpec((None, 1, d), mod_map(0)),
            pl.BlockSpec((None, 1, d), mod_map(1)),
            pl.BlockSpec((None, 1, d), lambda i, j: (layer, 0, 0)),
            pl.BlockSpec((None, d, tn), lambda i, j: (layer, 0, j)),
            pl.BlockSpec((None, d, 128), lambda i, j: (layer, 0, 0)),
        ],
        out_specs=[
            pl.BlockSpec((tm, tn), lambda i, j: (i, j)),
            pl.BlockSpec((tm, 128), lambda i, j: (i, 0)),
        ],
        out_shape=[jax.ShapeDtypeStruct((m, Z_COLS), BF16), jax.ShapeDtypeStruct((m, 128), F32)],
        scratch_shapes=[pltpu.VMEM((tm, d), BF16)],
        compiler_params=_params("arbitrary", "arbitrary"),
        name="in_projection",
    )(x, mod, mod, pre_w, w_main, w_ab)


def _rope_tables(t_len):
    half = HEAD_DIM // 2
    t = np.arange(t_len)
    pos = np.stack([t // GRID_W, t % GRID_W], axis=-1).astype(np.float64)
    inv_freq = ROPE_THETA ** (-np.arange(0, half, 2, dtype=np.float64) / half)
    ang = pos[..., None] * inv_freq
    ang = np.concatenate([ang, ang], axis=-1).reshape(t_len, HEAD_DIM)
    first = (np.arange(HEAD_DIM) % half) < half // 2
    cos = np.cos(ang)
    sin = np.sin(ang)
    sin_up = np.where(first, -sin, 0.0)
    sin_dn = np.where(first, 0.0, sin)
    return tuple(jnp.asarray(a, F32) for a in (cos, sin_up, sin_dn))


def _rope(x, cos, sin_up, sin_dn):
    q = HEAD_DIM // 4
    return x * cos + pltpu.roll(x, HEAD_DIM - q, 1) * sin_up + pltpu.roll(x, q, 1) * sin_dn


def _rms(x, w):
    return x * lax.rsqrt(jnp.mean(x * x, axis=-1, keepdims=True) + NORM_EPS) * w


def _keyprep_kernel(*refs, rope, n_fk):
    if rope:
        gk_ref, fk_ref, kw_ref, cos_ref, su_ref, sd_ref, gko_ref, fko_ref = refs
        tabs = (cos_ref[...], su_ref[...], sd_ref[...])
    else:
        gk_ref, kw_ref, gko_ref = refs
    kw = kw_ref[...]
    for j in range(GQA_KV_HEADS):
        sl = slice(j * HEAD_DIM, (j + 1) * HEAD_DIM)
        k = _rms(gk_ref[:, sl].astype(F32), kw)
        if rope:
            k = _rope(k, *tabs)
        gko_ref[:, sl] = k.astype(gko_ref.dtype)
    if rope:
        for j in range(n_fk):
            sl = slice(j * HEAD_DIM, (j + 1) * HEAD_DIM)
            fko_ref[:, sl] = _rope(fk_ref[:, sl].astype(F32), *tabs).astype(fko_ref.dtype)


def _key_prep(z, t_len, layer, k_norm_w, rope):
    m = z.shape[0]
    tm = min(256, t_len)
    tiles_per_seq = t_len // tm
    kw_spec = pl.BlockSpec((None, 1, HEAD_DIM), lambda i: (layer, 0, 0))
    gk_spec = pl.BlockSpec((tm, GQA_KV_W), lambda i: (i, OFF_GK * HEAD_DIM // GQA_KV_W))
    if rope:
        tabs = _rope_tables(t_len)
        tab_spec = pl.BlockSpec((tm, HEAD_DIM), lambda i: (i % tiles_per_seq, 0))
        return pl.pallas_call(
            functools.partial(_keyprep_kernel, rope=True, n_fk=2 * DIFF_HEADS),
            grid=(m // tm,),
            in_specs=[gk_spec, pl.BlockSpec((tm, DIFF_W), lambda i: (i, OFF_FK * HEAD_DIM // DIFF_W)),
                      kw_spec, tab_spec, tab_spec, tab_spec],
            out_specs=[pl.BlockSpec((tm, GQA_KV_W), lambda i: (i, 0)),
                       pl.BlockSpec((tm, DIFF_W), lambda i: (i, 0))],
            out_shape=[jax.ShapeDtypeStruct((m, GQA_KV_W), BF16), jax.ShapeDtypeStruct((m, DIFF_W), BF16)],
            compiler_params=_params("arbitrary"),
            name="key_prep_rope",
        )(z, z, k_norm_w, *tabs)
    return pl.pallas_call(
        functools.partial(_keyprep_kernel, rope=False, n_fk=0),
        grid=(m // tm,),
        in_specs=[gk_spec, kw_spec],
        out_specs=pl.BlockSpec((tm, GQA_KV_W), lambda i: (i, 0)),
        out_shape=jax.ShapeDtypeStruct((m, GQA_KV_W), F32),
        compiler_params=_params("arbitrary"),
        name="key_prep",
    )(z, k_norm_w)


def _fill_vt(v_ref, vt_ref, tk):
    def body(c, carry):
        r0 = pl.multiple_of(c * tk, tk)
        vt_ref[c] = jnp.transpose(v_ref[pl.ds(r0, tk), :].astype(F32)).astype(BF16)
        return carry

    lax.fori_loop(0, v_ref.shape[0] // tk, body, 0)


def _attend_t(q, k_ref, vt_ref, ck_ref, cvt_ref, s_refs, m_ref, l_ref, acc_ref, tk):
    m_ref[...] = jnp.full(m_ref.shape, -jnp.inf, F32)
    l_ref[...] = jnp.zeros(l_ref.shape, F32)
    acc_ref[...] = jnp.zeros(acc_ref.shape, F32)
    s_a, s_b = s_refs
    n_chunks = k_ref.shape[0] // tk

    def scores(k, s_ref):
        s_ref[0:k.shape[0], :] = _bdot_nt(k, q)

    def softmax_pv(s_ref, vt):
        n = vt.shape[1]
        m_prev = m_ref[...]
        m_new = jnp.maximum(m_prev, jnp.max(s_ref[0:n, :], axis=0, keepdims=True))
        alpha = jnp.exp2(m_prev - m_new)
        p = jnp.exp2(s_ref[0:n, :] - m_new)
        l_ref[...] = alpha * l_ref[...] + jnp.sum(p, axis=0, keepdims=True)
        acc_ref[...] = acc_ref[...] * alpha + jnp.dot(vt, p.astype(BF16), preferred_element_type=F32)
        m_ref[...] = m_new

    def k_chunk(c):
        return k_ref[pl.ds(pl.multiple_of(c * tk, tk), tk), :]

    scores(k_chunk(0), s_a)
    if n_chunks > 1:
        assert n_chunks % 2 == 0

        def body(i, carry):
            scores(k_chunk(2 * i + 1), s_b)
            softmax_pv(s_a, vt_ref[2 * i])
            scores(k_chunk(2 * i + 2), s_a)
            softmax_pv(s_b, vt_ref[2 * i + 1])
            return carry

        lax.fori_loop(0, n_chunks // 2 - 1, body, 0)
        scores(k_chunk(n_chunks - 1), s_b)
        softmax_pv(s_a, vt_ref[n_chunks - 2])
        if ck_ref is not None:
            scores(ck_ref[...], s_a)
        softmax_pv(s_b, vt_ref[n_chunks - 1])
        if ck_ref is not None:
            softmax_pv(s_a, cvt_ref[...])
    else:
        softmax_pv(s_a, vt_ref[0])
        if ck_ref is not None:
            scores(ck_ref[...], s_b)
            softmax_pv(s_b, cvt_ref[...])
    return acc_ref[...] / l_ref[...]


def _gqa_kernel(*refs, rope, cached, tq, tk):
    refs = list(refs)
    q_ref, g_ref, k_ref, v_ref, qw_ref = refs[:5]
    pos = 5
    ck_ref = cv_ref = cvt_s = None
    if cached:
        ck_ref, cv_ref = refs[pos:pos + 2]
        pos += 2
    if rope:
        tabs = tuple(r[...] for r in refs[pos:pos + 3])
        pos += 3
    o_ref, q_scr, vt_s = refs[pos:pos + 3]
    pos += 3
    if cached:
        cvt_s = refs[pos]
        pos += 1
    s_a, s_b, m_ref, l_ref, acc_ref = refs[pos:]

    @pl.when(pl.program_id(2) == 0)
    def _():
        _fill_vt(v_ref, vt_s, tk)
        if cached:
            cvt_s[...] = jnp.transpose(cv_ref[...]).astype(BF16)

    qw = qw_ref[...]
    scale = HEAD_DIM ** -0.5 * LOG2E
    for j in range(GQA_GROUP):
        qj = _rms(q_ref[:, j * HEAD_DIM:(j + 1) * HEAD_DIM].astype(F32), qw)
        if rope:
            qj = _rope(qj, *tabs)
        q_scr[j * tq:(j + 1) * tq, :] = (qj * scale).astype(BF16)
    o = jnp.transpose(_attend_t(q_scr[...], k_ref, vt_s, ck_ref, cvt_s, (s_a, s_b), m_ref, l_ref, acc_ref, tk))
    for j in range(GQA_GROUP):
        sl = slice(j * HEAD_DIM, (j + 1) * HEAD_DIM)
        o_ref[:, sl] = (o[j * tq:(j + 1) * tq, :] * _silu(g_ref[:, sl].astype(F32))).astype(o_ref.dtype)


def _gqa_attention(z, keys, t_len, layer, q_norm_w, cache_k, cache_v, rope):
    m = z.shape[0]
    n_seq = m // t_len
    tq = min(256, t_len)
    tk = min(512, t_len)
    nq = t_len // tq
    cached = cache_k is not None
    qcol = OFF_GQ * HEAD_DIM // (GQA_GROUP * HEAD_DIM)
    gcol = OFF_GG * HEAD_DIM // (GQA_GROUP * HEAD_DIM)
    in_specs = [
        pl.BlockSpec((tq, GQA_GROUP * HEAD_DIM), lambda b, g, i: (b * nq + i, qcol + g)),
        pl.BlockSpec((tq, GQA_GROUP * HEAD_DIM), lambda b, g, i: (b * nq + i, gcol + g)),
        pl.BlockSpec((t_len, HEAD_DIM), lambda b, g, i: (b, g)),
        pl.BlockSpec((t_len, HEAD_DIM), lambda b, g, i: (b, OFF_GV + g)),
        pl.BlockSpec((None, 1, HEAD_DIM), lambda b, g, i: (layer, 0, 0)),
    ]
    args = [z, z, keys, z, q_norm_w]
    rows = GQA_GROUP * tq
    scratch = [pltpu.VMEM((rows, HEAD_DIM), BF16), pltpu.VMEM((t_len // tk, HEAD_DIM, tk), BF16)]
    if cached:
        p_len = cache_k.shape[2]
        spec = pl.BlockSpec((None, None, p_len, HEAD_DIM), lambda b, g, i: (b, layer, 0, g))
        in_specs += [spec, spec]
        args += [cache_k, cache_v]
        scratch.append(pltpu.VMEM((HEAD_DIM, p_len), BF16))
    if rope:
        spec = pl.BlockSpec((tq, HEAD_DIM), lambda b, g, i: (i, 0))
        in_specs += [spec, spec, spec]
        args += list(_rope_tables(t_len))
    scratch += [pltpu.VMEM((tk, rows), F32), pltpu.VMEM((tk, rows), F32),
                pltpu.VMEM((1, rows), F32), pltpu.VMEM((1, rows), F32), pltpu.VMEM((HEAD_DIM, rows), F32)]
    return pl.pallas_call(
        functools.partial(_gqa_kernel, rope=rope, cached=cached, tq=tq, tk=tk),
        grid=(n_seq, GQA_KV_HEADS, nq),
        in_specs=in_specs,
        out_specs=pl.BlockSpec((tq, GQA_GROUP * HEAD_DIM), lambda b, g, i: (b * nq + i, g)),
        out_shape=jax.ShapeDtypeStruct((m, GQA_W), BF16),
        scratch_shapes=scratch,
        compiler_params=_params("arbitrary", "arbitrary", "arbitrary"),
        name="gqa_attention",
    )(*args)


def _diff_kernel(*refs, rope, cached, tk, lambda_init):
    refs = list(refs)
    q_ref, g_ref, k_ref, v_ref, lam_ref, nw_ref = refs[:6]
    pos = 6
    ck_ref = cv_ref = cvt_s = None
    if cached:
        ck_ref, cv_ref = refs[pos:pos + 2]
        pos += 2
    if rope:
        tabs = tuple(r[...] for r in refs[pos:pos + 3])
        pos += 3
    o_ref, vt_s = refs[pos:pos + 2]
    pos += 2
    if cached:
        cvt_s = refs[pos]
        pos += 1
    o1_s, s_a, s_b, m_ref, l_ref, acc_ref = refs[pos:]

    @pl.when(pl.program_id(2) == 0)
    def _():
        _fill_vt(v_ref, vt_s, tk)
        if cached:
            cvt_s[...] = jnp.transpose(cv_ref[...]).astype(BF16)

    scale = HEAD_DIM ** -0.5 * LOG2E
    lp = lam_ref[...]
    lam = (jnp.exp(jnp.sum(lp[0:1] * lp[1:2], axis=-1, keepdims=True))
           - jnp.exp(jnp.sum(lp[2:3] * lp[3:4], axis=-1, keepdims=True)) + lambda_init)
    for i in range(2):
        sl = slice(i * HEAD_DIM, (i + 1) * HEAD_DIM)
        qi = q_ref[:, sl].astype(F32)
        if rope:
            qi = _rope(qi, *tabs)
        qi = (qi * scale).astype(BF16)
        o_t = _attend_t(qi, k_ref.at[:, sl], vt_s, None if ck_ref is None else ck_ref.at[:, sl],
                        cvt_s, (s_a, s_b), m_ref, l_ref, acc_ref, tk)
        if i == 0:
            o1_s[...] = o_t
    o = jnp.transpose(o1_s[...] - lam * o_t)
    o = _rms(o, nw_ref[...]) * (1.0 - lambda_init)
    o_ref[...] = (o * _silu(g_ref[...].astype(F32))).astype(o_ref.dtype)


def _diff_attention(z, keys, key_col0, t_len, layer, lambda_init, diff_lambda, diff_norm_w,
                    cache_k, cache_v, rope):
    m = z.shape[0]
    n_seq = m // t_len
    tq = min(512, t_len)
    tk = min(512, t_len)
    nq = t_len // tq
    hw = 2 * HEAD_DIM
    cached = cache_k is not None
    in_specs = [
        pl.BlockSpec((tq, hw), lambda b, h, i: (b * nq + i, OFF_FQ // 2 + h)),
        pl.BlockSpec((tq, hw), lambda b, h, i: (b * nq + i, OFF_FG // 2 + h)),
        pl.BlockSpec((t_len, hw), lambda b, h, i: (b, key_col0 + h)),
        pl.BlockSpec((t_len, hw), lambda b, h, i: (b, OFF_FV // 2 + h)),
        pl.BlockSpec((None, 4, HEAD_DIM), lambda b, h, i: (layer, 0, 0)),
        pl.BlockSpec((None, 1, hw), lambda b, h, i: (layer, 0, 0)),
    ]
    args = [z, z, keys, z, diff_lambda, diff_norm_w]
    scratch = [pltpu.VMEM((t_len // tk, hw, tk), BF16)]
    if cached:
        p_len = cache_k.shape[2]
        spec = pl.BlockSpec((None, None, p_len, hw), lambda b, h, i: (b, layer, 0, h))
        in_specs += [spec, spec]
        args += [cache_k, cache_v]
        scratch.append(pltpu.VMEM((hw, p_len), BF16))
    if rope:
        spec = pl.BlockSpec((tq, HEAD_DIM), lambda b, h, i: (i, 0))
        in_specs += [spec, spec, spec]
        args += list(_rope_tables(t_len))
    scratch += [pltpu.VMEM((hw, tq), F32), pltpu.VMEM((tk, tq), F32), pltpu.VMEM((tk, tq), F32),
                pltpu.VMEM((1, tq), F32), pltpu.VMEM((1, tq), F32), pltpu.VMEM((hw, tq), F32)]
    return pl.pallas_call(
        functools.partial(_diff_kernel, rope=rope, cached=cached, tk=tk, lambda_init=lambda_init),
        grid=(n_seq, DIFF_HEADS, nq),
        in_specs=in_specs,
        out_specs=pl.BlockSpec((tq, hw), lambda b, h, i: (b * nq + i, h)),
        out_shape=jax.ShapeDtypeStruct((m, DIFF_W), BF16),
        scratch_shapes=scratch,
        compiler_params=_params("arbitrary", "arbitrary", "arbitrary"),
        name="diff_attention",
    )(*args)


def _split_bf16(a):
    hi = a.astype(BF16)
    return hi, (a - hi.astype(F32)).astype(BF16)


def _mm3(a_parts, b_parts):
    a_hi, a_lo = a_parts
    b_hi, b_lo = b_parts
    return jnp.dot(jnp.concatenate([a_hi, a_hi, a_lo], axis=1), jnp.concatenate([b_hi, b_lo, b_hi], axis=0),
                   preferred_element_type=F32)


def _interleave(*gens):
    gens = list(gens)
    while gens:
        for g in list(gens):
            try:
                next(g)
            except StopIteration:
                gens.remove(g)


def _chunk_cumsum(x, reverse):
    n = x.shape[0]
    row = lax.broadcasted_iota(jnp.int32, x.shape, 0)
    s = 1
    while s < n:
        if reverse:
            x = x + jnp.where(row < n - s, pltpu.roll(x, n - s, 0), 0.0)
        else:
            x = x + jnp.where(row >= s, pltpu.roll(x, s, 0), 0.0)
        s *= 2
    return x


def _dn_kernel(*refs, t_len, has_state):
    refs = list(refs)
    (zq_ref, zk_ref, zv_ref, zg_ref, ab_ref, cwq_ref, cwk_ref, cwv_ref,
     alog_ref, dtb_ref, nw_ref) = refs[:11]
    pos = 11
    s0_ref = None
    if has_state:
        s0_ref = refs[pos]
        pos += 1
    o_ref, sfin_ref = refs[pos:pos + 2]
    raw_s, qn_s, kn_s, vn_s, s_s, o_s = refs[pos + 2:pos + 8]
    n_set = 6
    sets = [refs[pos + 8:pos + 8 + n_set], refs[pos + 8 + n_set:pos + 8 + 2 * n_set]]
    c_len = DN_CHUNK
    n_chunks = t_len // c_len
    n_groups = n_chunks // PRE_UNROLL
    pad = 8
    blk = min(DN_BLOCK, t_len)
    head = pl.program_id(1)

    def l2n(x):
        return x * lax.rsqrt(jnp.sum(x * x, axis=-1, keepdims=True) + NORM_EPS)

    zero_pad = jnp.zeros((pad, HEAD_DIM), F32)
    for src, cw_ref, dst, post in ((zq_ref, cwq_ref, qn_s, lambda x: l2n(x) * (HEAD_DIM ** -0.5)),
                                   (zk_ref, cwk_ref, kn_s, l2n), (zv_ref, cwv_ref, vn_s, lambda x: x)):
        raw_s[0:pad, :] = zero_pad
        raw_s[pad + t_len:2 * pad + t_len, :] = zero_pad

        def fill(b, carry, src=src):
            r0 = pl.multiple_of(b * blk, blk)
            raw_s[pl.ds(r0 + pad, blk), :] = src[pl.ds(r0, blk), :].astype(F32)
            return carry

        lax.fori_loop(0, t_len // blk, fill, 0)
        cw = cw_ref[...]

        def conv(b, carry, dst=dst, post=post, cw=cw):
            r0 = pl.multiple_of(b * blk, blk)
            n = blk + 2 * pad
            win = raw_s[pl.ds(r0, n), :]
            prev = pltpu.roll(win, 1, 0)[pad:pad + blk]
            nxt = pltpu.roll(win, n - 1, 0)[pad:pad + blk]
            y = _silu(prev * cw[0:1] + win[pad:pad + blk] * cw[1:2] + nxt * cw[2:3])
            dst[pl.ds(r0, blk), :] = post(y)
            return carry

        lax.fori_loop(0, t_len // blk, conv, 0)

    lane = lax.broadcasted_iota(jnp.int32, (c_len, 128), 1)
    row2 = lax.broadcasted_iota(jnp.int32, (2 * c_len, 2 * c_len), 0)
    col2 = lax.broadcasted_iota(jnp.int32, (2 * c_len, 2 * c_len), 1)
    fwd_blk = (row2 < c_len) & (col2 < c_len)
    bwd_blk = (row2 >= c_len) & (col2 >= c_len)
    incl = (fwd_blk & (row2 >= col2)) | (bwd_blk & (row2 <= col2))
    strict = (fwd_blk & (row2 > col2)) | (bwd_blk & (row2 < col2))
    eye = (row2 == col2).astype(F32)
    neg_a = -jnp.exp(alog_ref[...])
    dtb = dtb_ref[...]
    zeros_v = jnp.zeros((c_len, HEAD_DIM), BF16)

    def pick(x, col):
        return jnp.broadcast_to(jnp.sum(jnp.where(lane == col, x, 0.0), axis=-1, keepdims=True),
                                (c_len, HEAD_DIM))

    def rows_of(i):
        return (pl.multiple_of(i * c_len, c_len), pl.multiple_of((n_chunks - 1 - i) * c_len, c_len))

    def pair(ref, i):
        rf, rb = rows_of(i)
        return jnp.concatenate([ref[pl.ds(rf, c_len), :], ref[pl.ds(rb, c_len), :]], axis=0)

    def pre_instance(i, j, bufs):
        u_b, wq_b, qk_b, kdt_b, et_b, rhs_b = bufs
        rf, rb = rows_of(i)
        q2, k2, v2 = pair(qn_s, i), pair(kn_s, i), pair(vn_s, i)
        gs, betas = [], []
        for d, r0 in enumerate((rf, rb)):
            ab = ab_ref[pl.ds(r0, c_len), :]
            x = ab + dtb
            g_all = neg_a * (jnp.maximum(x, 0.0) + jnp.log1p(jnp.exp(-jnp.abs(x))))
            gs.append(pick(g_all, d * DN_HEADS + head))
            betas.append(pick(jax.nn.sigmoid(ab), (2 + d) * DN_HEADS + head))
        beta2 = jnp.concatenate(betas, axis=0)
        gc2 = jnp.concatenate([_chunk_cumsum(gs[0], False), _chunk_cumsum(gs[1], True)], axis=0)
        tots = [jnp.sum(g, axis=0, keepdims=True) for g in gs]
        tot2 = jnp.concatenate([jnp.broadcast_to(t, (c_len, HEAD_DIM)) for t in tots], axis=0)
        k2b = k2.astype(BF16)
        kk2 = _bdot_nt(k2b, k2b)
        qk2 = _bdot_nt(q2, k2b)
        decay = jnp.where(incl, jnp.exp(jnp.where(incl, gc2 - jnp.transpose(gc2), 0.0)), 0.0)
        e2 = jnp.exp(gc2)
        rhs_b[j] = jnp.concatenate([v2 * beta2, k2 * beta2 * e2], axis=-1).astype(BF16)
        qk_b[j] = (qk2 * decay).astype(BF16)
        wq_b[j, 2 * c_len:4 * c_len, :] = (q2 * e2).astype(BF16)
        kdt_b[j] = jnp.transpose(k2 * jnp.exp(tot2 - gc2)).astype(BF16)
        et_b[j] = jnp.concatenate([jnp.broadcast_to(jnp.exp(t), (8, HEAD_DIM)) for t in tots], axis=1)
        return jnp.where(strict, kk2 * beta2 * decay, 0.0)

    def pre_group(g, bufs):
        u_b, wq_b, qk_b, kdt_b, et_b, rhs_b = bufs
        l_mats = []
        for j in range(PRE_UNROLL):
            l_mats.append(pre_instance(PRE_UNROLL * g + j, j, bufs))
            yield
        ts = [eye - l for l in l_mats]
        l_parts = [_split_bf16(l) for l in l_mats]
        ps = [_mm3(lp, lp) for lp in l_parts]
        yield
        steps = int(math.log2(DN_CHUNK)) - 1
        for s in range(steps):
            p_parts = [_split_bf16(p) for p in ps]
            ts = [t + _mm3(_split_bf16(t), pp) for t, pp in zip(ts, p_parts)]
            yield
            if s + 1 < steps:
                ps = [_mm3(pp, pp) for pp in p_parts]
                yield
        for j in range(PRE_UNROLL):
            uw = jnp.dot(ts[j].astype(BF16), rhs_b[j], preferred_element_type=F32)
            u_b[j] = uw[:, 0:HEAD_DIM]
            wq_b[j, 0:2 * c_len, :] = uw[:, HEAD_DIM:].astype(BF16)
            yield

    def scan_group(g, bufs):
        u_b, wq_b, qk_b, kdt_b, et_b, rhs_b = bufs
        for j in range(PRE_UNROLL):
            rf, rb = rows_of(PRE_UNROLL * g + j)
            s2 = s_s[...]
            ws = jnp.dot(wq_b[j], s2.astype(BF16), preferred_element_type=F32)
            yield
            u2 = u_b[j]
            vb_f = (u2[0:c_len] - ws[0:c_len, 0:HEAD_DIM]).astype(BF16)
            vb_b = (u2[c_len:] - ws[c_len:2 * c_len, HEAD_DIM:]).astype(BF16)
            o2 = jnp.dot(qk_b[j], jnp.concatenate([vb_f, vb_b], axis=0), preferred_element_type=F32)
            o_s[0, pl.ds(rf, c_len), :] = ws[2 * c_len:3 * c_len, 0:HEAD_DIM] + o2[0:c_len]
            o_s[1, pl.ds(rb, c_len), :] = ws[3 * c_len:, HEAD_DIM:] + o2[c_len:]
            v_bd = jnp.concatenate([jnp.concatenate([vb_f, zeros_v], axis=1),
                                    jnp.concatenate([zeros_v, vb_b], axis=1)], axis=0)
            s_s[...] = s2 * et_b[j][0:1, :] + jnp.dot(kdt_b[j], v_bd, preferred_element_type=F32)
            yield

    if has_state:
        s_s[...] = jnp.concatenate([s0_ref[0], s0_ref[1]], axis=1).astype(F32)
    else:
        s_s[...] = jnp.zeros(s_s.shape, F32)

    _interleave(pre_group(0, sets[0]))
    if n_groups > 1:
        assert n_groups % 2 == 0

        def body(h, carry):
            _interleave(pre_group(2 * h + 1, sets[1]), scan_group(2 * h, sets[0]))
            _interleave(pre_group(2 * h + 2, sets[0]), scan_group(2 * h + 1, sets[1]))
            return carry

        lax.fori_loop(0, n_groups // 2 - 1, body, 0)
        _interleave(pre_group(n_groups - 1, sets[1]), scan_group(n_groups - 2, sets[0]))
        _interleave(scan_group(n_groups - 1, sets[1]))
    else:
        _interleave(scan_group(0, sets[0]))
    s_fin = s_s[...]
    sfin_ref[0] = s_fin[:, 0:HEAD_DIM].astype(sfin_ref.dtype)
    sfin_ref[1] = s_fin[:, HEAD_DIM:].astype(sfin_ref.dtype)

    nw = nw_ref[...]

    def finish(b, carry):
        r0 = pl.multiple_of(b * blk, blk)
        o = o_s[0, pl.ds(r0, blk), :] + o_s[1, pl.ds(r0, blk), :]
        o = _rms(o, nw) * _silu(zg_ref[pl.ds(r0, blk), :].astype(F32))
        o_ref[pl.ds(r0, blk), :] = o.astype(o_ref.dtype)
        return carry

    lax.fori_loop(0, t_len // blk, finish, 0)


def _delta_net(z, ab, t_len, layer, conv_w, a_log_row, dt_bias_row, dn_norm_w, state):
    m = z.shape[0]
    n_seq = m // t_len
    n_chunks = t_len // DN_CHUNK
    assert n_chunks % PRE_UNROLL == 0 and t_len % min(DN_BLOCK, t_len) == 0
    has_state = state is not None

    def zspec(off):
        return pl.BlockSpec((t_len, HEAD_DIM), lambda b, h: (b, off + h))

    def cwspec(off):
        return pl.BlockSpec((None, CONV_K, HEAD_DIM), lambda b, h: (layer, 0, off + h))

    row_spec = pl.BlockSpec((None, 1, 128), lambda b, h: (layer, 0, 0))
    in_specs = [zspec(OFF_DQ), zspec(OFF_DK), zspec(OFF_DV), zspec(OFF_DG),
                pl.BlockSpec((t_len, 128), lambda b, h: (b, 0)),
                cwspec(0), cwspec(DN_HEADS), cwspec(2 * DN_HEADS),
                row_spec, row_spec, row_spec]
    args = [z, z, z, z, ab, conv_w, conv_w, conv_w, a_log_row, dt_bias_row, dn_norm_w]
    if has_state:
        in_specs.append(pl.BlockSpec((None, None, 2, None, HEAD_DIM, HEAD_DIM),
                                     lambda b, h: (b, layer, 0, h, 0, 0)))
        args.append(state)
    c2 = 2 * DN_CHUNK
    group_set = [
        pltpu.VMEM((PRE_UNROLL, c2, HEAD_DIM), F32),
        pltpu.VMEM((PRE_UNROLL, 2 * c2, HEAD_DIM), BF16),
        pltpu.VMEM((PRE_UNROLL, c2, c2), BF16),
        pltpu.VMEM((PRE_UNROLL, HEAD_DIM, c2), BF16),
        pltpu.VMEM((PRE_UNROLL, 8, 2 * HEAD_DIM), F32),
        pltpu.VMEM((PRE_UNROLL, c2, 2 * HEAD_DIM), BF16),
    ]
    scratch = [
        pltpu.VMEM((t_len + 16, HEAD_DIM), F32),
        pltpu.VMEM((t_len, HEAD_DIM), F32),
        pltpu.VMEM((t_len, HEAD_DIM), F32),
        pltpu.VMEM((t_len, HEAD_DIM), F32),
        pltpu.VMEM((HEAD_DIM, 2 * HEAD_DIM), F32),
        pltpu.VMEM((2, t_len, HEAD_DIM), F32),
    ] + group_set + group_set
    return pl.pallas_call(
        functools.partial(_dn_kernel, t_len=t_len, has_state=has_state),
        grid=(n_seq, DN_HEADS),
        in_specs=in_specs,
        out_specs=[pl.BlockSpec((t_len, HEAD_DIM), lambda b, h: (b, h)),
                   pl.BlockSpec((None, 2, None, HEAD_DIM, HEAD_DIM), lambda b, h: (b, 0, h, 0, 0))],
        out_shape=[jax.ShapeDtypeStruct((m, DN_W), BF16),
                   jax.ShapeDtypeStruct((n_seq, 2, DN_HEADS, HEAD_DIM, HEAD_DIM), F32)],
        scratch_shapes=scratch,
        compiler_params=_params("arbitrary", "arbitrary"),
        name="delta_net",
    )(*args)


def _outproj_kernel(a1_ref, a2_ref, a3_ref, w_ref, o_ref):
    acc = jnp.dot(a1_ref[...], w_ref[0:DN_W, :], preferred_element_type=F32)
    acc += jnp.dot(a2_ref[...], w_ref[DN_W:DN_W + GQA_W, :], preferred_element_type=F32)
    acc += jnp.dot(a3_ref[...], w_ref[DN_W + GQA_W:MIX_W, :], preferred_element_type=F32)
    o_ref[...] = acc.astype(o_ref.dtype)


def _out_projection(dn, gqa, diff, w_out, layer):
    m = dn.shape[0]
    d = w_out.shape[-1]
    tm = min(1024, m)
    tn = min(1024, d)
    return pl.pallas_call(
        _outproj_kernel,
        grid=(m // tm, d // tn),
        in_specs=[pl.BlockSpec((tm, DN_W), lambda i, j: (i, 0)),
                  pl.BlockSpec((tm, GQA_W), lambda i, j: (i, 0)),
                  pl.BlockSpec((tm, DIFF_W), lambda i, j: (i, 0)),
                  pl.BlockSpec((None, MIX_W, tn), lambda i, j: (layer, 0, j))],
        out_specs=pl.BlockSpec((tm, tn), lambda i, j: (i, j)),
        out_shape=jax.ShapeDtypeStruct((m, d), BF16),
        compiler_params=_params("arbitrary", "arbitrary"),
        name="out_projection",
    )(dn, gqa, diff, w_out)


def _residual_kernel(x_ref, mix_ref, g_ref, pw_ref, o_ref):
    o_ref[...] = x_ref[...] + g_ref[...] * _rms(mix_ref[...].astype(F32), pw_ref[...])


def _post_residual(x, mix, mod, mod_row0, rows_per_mod, layer, post_w):
    m, d = x.shape
    tm = min(256, rows_per_mod)
    tiles_per_mod = rows_per_mod // tm
    base = layer * MOD_ROWS + mod_row0
    return pl.pallas_call(
        _residual_kernel,
        grid=(m // tm,),
        in_specs=[pl.BlockSpec((tm, d), lambda i: (i, 0)),
                  pl.BlockSpec((tm, d), lambda i: (i, 0)),
                  pl.BlockSpec((None, 1, d), lambda i: (base + i // tiles_per_mod, 0, 2)),
                  pl.BlockSpec((None, 1, d), lambda i: (layer, 0, 0))],
        out_specs=pl.BlockSpec((tm, d), lambda i: (i, 0)),
        out_shape=jax.ShapeDtypeStruct((m, d), F32),
        compiler_params=_params("arbitrary"),
        name="post_residual",
    )(x, mix, mod, post_w)


def _mixer_layer(x, t_len, mod, mod_row0, rows_per_mod, layer, lambda_init, wts, caches):
    (pre_w, post_w, w_main, w_ab, w_out, conv_w, a_log_row, dt_bias_row, dn_norm_w,
     q_norm_w, k_norm_w, diff_lambda, diff_norm_w) = wts
    latent = caches is not None
    z, ab = _in_projection(x, mod, mod_row0, rows_per_mod, layer, pre_w, w_main, w_ab)
    if latent:
        cgk, cgv, cfk, cfv, state = caches
        gkeys, fkeys = _key_prep(z, t_len, layer, k_norm_w, rope=True)
        fkey_col0 = 0
    else:
        cgk = cgv = cfk = cfv = state = None
        gkeys = _key_prep(z, t_len, layer, k_norm_w, rope=False)
        fkeys, fkey_col0 = z, OFF_FK // 2
    dn, s_fin = _delta_net(z, ab, t_len, layer, conv_w, a_log_row, dt_bias_row, dn_norm_w, state)
    gqa = _gqa_attention(z, gkeys, t_len, layer, q_norm_w, cgk, cgv, rope=latent)
    diff = _diff_attention(z, fkeys, fkey_col0, t_len, layer, lambda_init, diff_lambda, diff_norm_w,
                           cfk, cfv, rope=latent)
    mix = _out_projection(dn, gqa, diff, w_out, layer)
    y = _post_residual(x, mix, mod, mod_row0, rows_per_mod, layer, post_w)
    return y, z, gkeys, s_fin


def kernel(x_prompt, x_sample, cache_gqa_k, cache_gqa_v, cache_diff_k, cache_diff_v, state_dn, c, c_ctx,
           w_mod, b_mod, pre_norm_w, post_norm_w, w_in, w_out, dn_conv_w, dn_a_log, dn_dt_bias, dn_norm_w,
           gqa_q_norm_w, gqa_k_norm_w, diff_lambda, diff_norm_w):
    n_ctx, t_ctx, d = x_prompt.shape
    n_lat, t_lat, _ = x_sample.shape
    depth = w_in.shape[0]
    p_len = cache_gqa_k.shape[2]
    assert n_lat + 1 <= MOD_ROWS

    n_dn = 4 * DN_W
    w_main = jnp.concatenate([w_in[:, :, :n_dn], w_in[:, :, n_dn + N_GATE_COLS:]], axis=-1).astype(BF16)
    w_ab = jnp.pad(w_in[:, :, n_dn:n_dn + N_GATE_COLS], ((0, 0), (0, 0), (0, 128 - N_GATE_COLS))).astype(BF16)
    w_out_b = w_out.astype(BF16)
    row128 = lambda a: jnp.pad(a.reshape(depth, 1, -1), ((0, 0), (0, 0), (0, 128 - a[0].size)))
    a_log_row = row128(dn_a_log)
    dt_bias_row = row128(dn_dt_bias)
    r3 = lambda a: a.reshape(depth, 1, a.shape[-1])

    cond = jnp.concatenate([c_ctx[None, :], c, jnp.zeros((MOD_ROWS - 1 - n_lat, d), F32)], axis=0)
    mod = _modulation(cond, w_mod, b_mod).reshape(depth * MOD_ROWS, 1, 3 * d)

    cgk = cache_gqa_k.reshape(n_lat, depth, p_len, GQA_KV_W)
    cgv = cache_gqa_v.reshape(n_lat, depth, p_len, GQA_KV_W)
    cfk = cache_diff_k.reshape(n_lat, depth, p_len, DIFF_W)
    cfv = cache_diff_v.reshape(n_lat, depth, p_len, DIFF_W)

    xp = x_prompt.reshape(n_ctx * t_ctx, d)
    xs = x_sample.reshape(n_lat * t_lat, d)
    new_k, new_v, new_dk, new_dv, new_s = [], [], [], [], []
    for l in range(depth):
        lambda_init = 0.8 - 0.6 * math.exp(-0.3 * l)
        wts = (r3(pre_norm_w), r3(post_norm_w), w_main, w_ab, w_out_b, dn_conv_w, a_log_row, dt_bias_row,
               r3(dn_norm_w), r3(gqa_q_norm_w), r3(gqa_k_norm_w), diff_lambda, r3(diff_norm_w))
        xp, z, gk, s_fin = _mixer_layer(xp, t_ctx, mod, 0, n_ctx * t_ctx, l, lambda_init, wts, None)
        zc = lambda off, n: z[:, off * HEAD_DIM:off * HEAD_DIM + n].astype(F32)
        new_k.append(gk.reshape(n_ctx, t_ctx, GQA_KV_HEADS, HEAD_DIM))
        new_v.append(zc(OFF_GV, GQA_KV_W).reshape(n_ctx, t_ctx, GQA_KV_HEADS, HEAD_DIM))
        new_dk.append(zc(OFF_FK, DIFF_W).reshape(n_ctx, t_ctx, DIFF_HEADS, 2, HEAD_DIM))
        new_dv.append(zc(OFF_FV, DIFF_W).reshape(n_ctx, t_ctx, DIFF_HEADS, 2 * HEAD_DIM))
        new_s.append(s_fin)
        xs, _, _, _ = _mixer_layer(xs, t_lat, mod, 1, t_lat, l, lambda_init, wts,
                                   (cgk, cgv, cfk, cfv, state_dn))
    return (xp.reshape(n_ctx, t_ctx, d), xs.reshape(n_lat, t_lat, d),
            jnp.stack(new_k, axis=1), jnp.stack(new_v, axis=1), jnp.stack(new_dk, axis=1),
            jnp.stack(new_dv, axis=1), jnp.stack(new_s, axis=1))
```

```python
import functools
import math

import numpy as np
import jax
import jax.numpy as jnp
from jax import lax
from jax.experimental import pallas as pl
from jax.experimental.pallas import tpu as pltpu

F32 = jnp.float32
BF16 = jnp.bfloat16

HEAD_DIM = 128
GRID_W = 64
ROPE_THETA = 10000.0
NORM_EPS = 1e-6
DN_HEADS = 8
DN_CHUNK = 64
CONV_K = 3
GQA_Q_HEADS = 16
GQA_KV_HEADS = 4
GQA_GROUP = GQA_Q_HEADS // GQA_KV_HEADS
DIFF_HEADS = 4
DN_W = DN_HEADS * HEAD_DIM
GQA_W = GQA_Q_HEADS * HEAD_DIM
GQA_KV_W = GQA_KV_HEADS * HEAD_DIM
DIFF_W = DIFF_HEADS * 2 * HEAD_DIM
MIX_W = DN_W + GQA_W + DIFF_W
N_GATE_COLS = 4 * DN_HEADS

OFF_DQ, OFF_DK, OFF_DV, OFF_DG = 0, 8, 16, 24
OFF_GQ, OFF_GK, OFF_GV, OFF_GG = 32, 48, 52, 56
OFF_FQ, OFF_FK, OFF_FV, OFF_FG = 72, 80, 88, 96
Z_BLOCKS = 104
Z_COLS = Z_BLOCKS * HEAD_DIM

VMEM_LIMIT_BYTES = 56 * 1024 * 1024
MOD_ROWS = 16
LOG2E = math.log2(math.e)
PRE_UNROLL = 4
DN_BLOCK = 512


def _params(*sem):
    return pltpu.CompilerParams(dimension_semantics=sem, vmem_limit_bytes=VMEM_LIMIT_BYTES)


def _silu(x):
    return x * jax.nn.sigmoid(x)


def _bdot(a, b):
    return jnp.dot(a.astype(BF16), b.astype(BF16), preferred_element_type=F32)


def _bdot_nt(a, b):
    return lax.dot_general(a.astype(BF16), b.astype(BF16), (((1,), (1,)), ((), ())),
                           preferred_element_type=F32)


def _mod_kernel(c_ref, w_ref, b_ref, o_ref):
    o_ref[...] = _bdot(_silu(c_ref[...]), w_ref[...]) + b_ref[...]


def _modulation(cond, w_mod, b_mod):
    n_layers, d, n = w_mod.shape
    tn = min(512, n)
    return pl.pallas_call(
        _mod_kernel,
        grid=(n_layers, n // tn),
        in_specs=[
            pl.BlockSpec((MOD_ROWS, d), lambda l, j: (0, 0)),
            pl.BlockSpec((None, d, tn), lambda l, j: (l, 0, j)),
            pl.BlockSpec((None, 1, tn), lambda l, j: (l, 0, j)),
        ],
        out_specs=pl.BlockSpec((None, MOD_ROWS, tn), lambda l, j: (l, 0, j)),
        out_shape=jax.ShapeDtypeStruct((n_layers, MOD_ROWS, n), F32),
        compiler_params=_params("arbitrary", "arbitrary"),
        name="modulation",
    )(cond, w_mod, b_mod.reshape(n_layers, 1, n))


def _inproj_kernel(x_ref, sh_ref, sc_ref, pw_ref, w_ref, wab_ref, z_ref, ab_ref, h_scr):
    @pl.when(pl.program_id(1) == 0)
    def _():
        x = x_ref[...]
        y = x * lax.rsqrt(jnp.mean(x * x, axis=-1, keepdims=True) + NORM_EPS) * pw_ref[...]
        h = (y * (1.0 + sc_ref[...]) + sh_ref[...]).astype(BF16)
        h_scr[...] = h
        ab_ref[...] = jnp.dot(h, wab_ref[...], preferred_element_type=F32)

    z_ref[...] = jnp.dot(h_scr[...], w_ref[...], preferred_element_type=F32).astype(z_ref.dtype)


def _in_projection(x, mod, mod_row0, rows_per_mod, layer, pre_w, w_main, w_ab):
    m, d = x.shape
    tm = min(512, rows_per_mod)
    tn = 1024
    assert rows_per_mod % tm == 0 and m % tm == 0 and Z_COLS % tn == 0
    tiles_per_mod = rows_per_mod // tm
    base = layer * MOD_ROWS + mod_row0

    def mod_map(part):
        return lambda i, j: (base + i // tiles_per_mod, 0, part)

    return pl.pallas_call(
        _inproj_kernel,
        grid=(m // tm, Z_COLS // tn),
        in_specs=[
            pl.BlockSpec((tm, d), lambda i, j: (i, 0)),
            pl.BlockSpec((None, 1, d), mod_map(0)),
            pl.BlockSpec((None, 1, d), mod_map(1)),
            pl.BlockSpec((None, 1, d), lambda i, j: (layer, 0, 0)),
            pl.BlockSpec((None, d, tn), lambda i, j: (layer, 0, j)),
            pl.BlockSpec((None, d, 128), lambda i, j: (layer, 0, 0)),
        ],
        out_specs=[
            pl.BlockSpec((tm, tn), lambda i, j: (i, j)),
            pl.BlockSpec((tm, 128), lambda i, j: (i, 0)),
        ],
        out_shape=[jax.ShapeDtypeStruct((m, Z_COLS), BF16), jax.ShapeDtypeStruct((m, 128), F32)],
        scratch_shapes=[pltpu.VMEM((tm, d), BF16)],
        compiler_params=_params("arbitrary", "arbitrary"),
        name="in_projection",
    )(x, mod, mod, pre_w, w_main, w_ab)


def _rope_tables(t_len):
    half = HEAD_DIM // 2
    t = np.arange(t_len)
    pos = np.stack([t // GRID_W, t % GRID_W], axis=-1).astype(np.float64)
    inv_freq = ROPE_THETA ** (-np.arange(0, half, 2, dtype=np.float64) / half)
    ang = pos[..., None] * inv_freq
    ang = np.concatenate([ang, ang], axis=-1).reshape(t_len, HEAD_DIM)
    first = (np.arange(HEAD_DIM) % half) < half // 2
    cos = np.cos(ang)
    sin = np.sin(ang)
    sin_up = np.where(first, -sin, 0.0)
    sin_dn = np.where(first, 0.0, sin)
    return tuple(jnp.asarray(a, F32) for a in (cos, sin_up, sin_dn))


def _rope(x, cos, sin_up, sin_dn):
    q = HEAD_DIM // 4
    return x * cos + pltpu.roll(x, HEAD_DIM - q, 1) * sin_up + pltpu.roll(x, q, 1) * sin_dn


def _rms(x, w):
    return x * lax.rsqrt(jnp.mean(x * x, axis=-1, keepdims=True) + NORM_EPS) * w


def _keyprep_kernel(*refs, rope, n_fk):
    if rope:
        gk_ref, fk_ref, kw_ref, cos_ref, su_ref, sd_ref, gko_ref, fko_ref = refs
        tabs = (cos_ref[...], su_ref[...], sd_ref[...])
    else:
        gk_ref, kw_ref, gko_ref = refs
    kw = kw_ref[...]
    for j in range(GQA_KV_HEADS):
        sl = slice(j * HEAD_DIM, (j + 1) * HEAD_DIM)
        k = _rms(gk_ref[:, sl].astype(F32), kw)
        if rope:
            k = _rope(k, *tabs)
        gko_ref[:, sl] = k.astype(gko_ref.dtype)
    if rope:
        for j in range(n_fk):
            sl = slice(j * HEAD_DIM, (j + 1) * HEAD_DIM)
            fko_ref[:, sl] = _rope(fk_ref[:, sl].astype(F32), *tabs).astype(fko_ref.dtype)


def _key_prep(z, t_len, layer, k_norm_w, rope):
    m = z.shape[0]
    tm = min(256, t_len)
    tiles_per_seq = t_len // tm
    kw_spec = pl.BlockSpec((None, 1, HEAD_DIM), lambda i: (layer, 0, 0))
    gk_spec = pl.BlockSpec((tm, GQA_KV_W), lambda i: (i, OFF_GK * HEAD_DIM // GQA_KV_W))
    if rope:
        tabs = _rope_tables(t_len)
        tab_spec = pl.BlockSpec((tm, HEAD_DIM), lambda i: (i % tiles_per_seq, 0))
        return pl.pallas_call(
            functools.partial(_keyprep_kernel, rope=True, n_fk=2 * DIFF_HEADS),
            grid=(m // tm,),
            in_specs=[gk_spec, pl.BlockSpec((tm, DIFF_W), lambda i: (i, OFF_FK * HEAD_DIM // DIFF_W)),
                      kw_spec, tab_spec, tab_spec, tab_spec],
            out_specs=[pl.BlockSpec((tm, GQA_KV_W), lambda i: (i, 0)),
                       pl.BlockSpec((tm, DIFF_W), lambda i: (i, 0))],
            out_shape=[jax.ShapeDtypeStruct((m, GQA_KV_W), BF16), jax.ShapeDtypeStruct((m, DIFF_W), BF16)],
            compiler_params=_params("arbitrary"),
            name="key_prep_rope",
        )(z, z, k_norm_w, *tabs)
    return pl.pallas_call(
        functools.partial(_keyprep_kernel, rope=False, n_fk=0),
        grid=(m // tm,),
        in_specs=[gk_spec, kw_spec],
        out_specs=pl.BlockSpec((tm, GQA_KV_W), lambda i: (i, 0)),
        out_shape=jax.ShapeDtypeStruct((m, GQA_KV_W), F32),
        compiler_params=_params("arbitrary"),
        name="key_prep",
    )(z, k_norm_w)


def _fill_vt(v_ref, vt_ref, tk):
    def body(c, carry):
        r0 = pl.multiple_of(c * tk, tk)
        vt_ref[c] = jnp.transpose(v_ref[pl.ds(r0, tk), :].astype(F32)).astype(BF16)
        return carry

    lax.fori_loop(0, v_ref.shape[0] // tk, body, 0)


def _attend_t(q, k_ref, vt_ref, ck_ref, cvt_ref, s_refs, m_ref, l_ref, acc_ref, tk):
    m_ref[...] = jnp.full(m_ref.shape, -jnp.inf, F32)
    l_ref[...] = jnp.zeros(l_ref.shape, F32)
    acc_ref[...] = jnp.zeros(acc_ref.shape, F32)
    s_a, s_b = s_refs
    n_chunks = k_ref.shape[0] // tk

    def scores(k, s_ref):
        s_ref[0:k.shape[0], :] = _bdot_nt(k, q)

    def softmax_pv(s_ref, vt):
        n = vt.shape[1]
        m_prev = m_ref[...]
        m_new = jnp.maximum(m_prev, jnp.max(s_ref[0:n, :], axis=0, keepdims=True))
        alpha = jnp.exp2(m_prev - m_new)
        p = jnp.exp2(s_ref[0:n, :] - m_new)
        l_ref[...] = alpha * l_ref[...] + jnp.sum(p, axis=0, keepdims=True)
        acc_ref[...] = acc_ref[...] * alpha + jnp.dot(vt, p.astype(BF16), preferred_element_type=F32)
        m_ref[...] = m_new

    def k_chunk(c):
        return k_ref[pl.ds(pl.multiple_of(c * tk, tk), tk), :]

    scores(k_chunk(0), s_a)
    if n_chunks > 1:
        assert n_chunks % 2 == 0

        def body(i, carry):
            scores(k_chunk(2 * i + 1), s_b)
            softmax_pv(s_a, vt_ref[2 * i])
            scores(k_chunk(2 * i + 2), s_a)
            softmax_pv(s_b, vt_ref[2 * i + 1])
            return carry

        lax.fori_loop(0, n_chunks // 2 - 1, body, 0)
        scores(k_chunk(n_chunks - 1), s_b)
        softmax_pv(s_a, vt_ref[n_chunks - 2])
        if ck_ref is not None:
            scores(ck_ref[...], s_a)
        softmax_pv(s_b, vt_ref[n_chunks - 1])
        if ck_ref is not None:
            softmax_pv(s_a, cvt_ref[...])
    else:
        softmax_pv(s_a, vt_ref[0])
        if ck_ref is not None:
            scores(ck_ref[...], s_b)
            softmax_pv(s_b, cvt_ref[...])
    return acc_ref[...] / l_ref[...]


def _gqa_kernel(*refs, rope, cached, tq, tk):
    refs = list(refs)
    q_ref, g_ref, k_ref, v_ref, qw_ref = refs[:5]
    pos = 5
    ck_ref = cv_ref = cvt_s = None
    if cached:
        ck_ref, cv_ref = refs[pos:pos + 2]
        pos += 2
    if rope:
        tabs = tuple(r[...] for r in refs[pos:pos + 3])
        pos += 3
    o_ref, q_scr, vt_s = refs[pos:pos + 3]
    pos += 3
    if cached:
        cvt_s = refs[pos]
        pos += 1
    s_a, s_b, m_ref, l_ref, acc_ref = refs[pos:]

    @pl.when(pl.program_id(2) == 0)
    def _():
        _fill_vt(v_ref, vt_s, tk)
        if cached:
            cvt_s[...] = jnp.transpose(cv_ref[...]).astype(BF16)

    qw = qw_ref[...]
    scale = HEAD_DIM ** -0.5 * LOG2E
    for j in range(GQA_GROUP):
        qj = _rms(q_ref[:, j * HEAD_DIM:(j + 1) * HEAD_DIM].astype(F32), qw)
        if rope:
            qj = _rope(qj, *tabs)
        q_scr[j * tq:(j + 1) * tq, :] = (qj * scale).astype(BF16)
    o = jnp.transpose(_attend_t(q_scr[...], k_ref, vt_s, ck_ref, cvt_s, (s_a, s_b), m_ref, l_ref, acc_ref, tk))
    for j in range(GQA_GROUP):
        sl = slice(j * HEAD_DIM, (j + 1) * HEAD_DIM)
        o_ref[:, sl] = (o[j * tq:(j + 1) * tq, :] * _silu(g_ref[:, sl].astype(F32))).astype(o_ref.dtype)


def _gqa_attention(z, keys, t_len, layer, q_norm_w, cache_k, cache_v, rope):
    m = z.shape[0]
    n_seq = m // t_len
    tq = min(256, t_len)
    tk = min(512, t_len)
    nq = t_len // tq
    cached = cache_k is not None
    qcol = OFF_GQ * HEAD_DIM // (GQA_GROUP * HEAD_DIM)
    gcol = OFF_GG * HEAD_DIM // (GQA_GROUP * HEAD_DIM)
    in_specs = [
        pl.BlockSpec((tq, GQA_GROUP * HEAD_DIM), lambda b, g, i: (b * nq + i, qcol + g)),
        pl.BlockSpec((tq, GQA_GROUP * HEAD_DIM), lambda b, g, i: (b * nq + i, gcol + g)),
        pl.BlockSpec((t_len, HEAD_DIM), lambda b, g, i: (b, g)),
        pl.BlockSpec((t_len, HEAD_DIM), lambda b, g, i: (b, OFF_GV + g)),
        pl.BlockSpec((None, 1, HEAD_DIM), lambda b, g, i: (layer, 0, 0)),
    ]
    args = [z, z, keys, z, q_norm_w]
    rows = GQA_GROUP * tq
    scratch = [pltpu.VMEM((rows, HEAD_DIM), BF16), pltpu.VMEM((t_len // tk, HEAD_DIM, tk), BF16)]
    if cached:
        p_len = cache_k.shape[2]
        spec = pl.BlockSpec((None, None, p_len, HEAD_DIM), lambda b, g, i: (b, layer, 0, g))
        in_specs += [spec, spec]
        args += [cache_k, cache_v]
        scratch.append(pltpu.VMEM((HEAD_DIM, p_len), BF16))
    if rope:
        spec = pl.BlockSpec((tq, HEAD_DIM), lambda b, g, i: (i, 0))
        in_specs += [spec, spec, spec]
        args += list(_rope_tables(t_len))
    scratch += [pltpu.VMEM((tk, rows), F32), pltpu.VMEM((tk, rows), F32),
                pltpu.VMEM((1, rows), F32), pltpu.VMEM((1, rows), F32), pltpu.VMEM((HEAD_DIM, rows), F32)]
    return pl.pallas_call(
        functools.partial(_gqa_kernel, rope=rope, cached=cached, tq=tq, tk=tk),
        grid=(n_seq, GQA_KV_HEADS, nq),
        in_specs=in_specs,
        out_specs=pl.BlockSpec((tq, GQA_GROUP * HEAD_DIM), lambda b, g, i: (b * nq + i, g)),
        out_shape=jax.ShapeDtypeStruct((m, GQA_W), BF16),
        scratch_shapes=scratch,
        compiler_params=_params("arbitrary", "arbitrary", "arbitrary"),
        name="gqa_attention",
    )(*args)


def _diff_kernel(*refs, rope, cached, tk, lambda_init):
    refs = list(refs)
    q_ref, g_ref, k_ref, v_ref, lam_ref, nw_ref = refs[:6]
    pos = 6
    ck_ref = cv_ref = cvt_s = None
    if cached:
        ck_ref, cv_ref = refs[pos:pos + 2]
        pos += 2
    if rope:
        tabs = tuple(r[...] for r in refs[pos:pos + 3])
        pos += 3
    o_ref, vt_s = refs[pos:pos + 2]
    pos += 2
    if cached:
        cvt_s = refs[pos]
        pos += 1
    o1_s, s_a, s_b, m_ref, l_ref, acc_ref = refs[pos:]

    @pl.when(pl.program_id(2) == 0)
    def _():
        _fill_vt(v_ref, vt_s, tk)
        if cached:
            cvt_s[...] = jnp.transpose(cv_ref[...]).astype(BF16)

    scale = HEAD_DIM ** -0.5 * LOG2E
    lp = lam_ref[...]
    lam = (jnp.exp(jnp.sum(lp[0:1] * lp[1:2], axis=-1, keepdims=True))
           - jnp.exp(jnp.sum(lp[2:3] * lp[3:4], axis=-1, keepdims=True)) + lambda_init)
    for i in range(2):
        sl = slice(i * HEAD_DIM, (i + 1) * HEAD_DIM)
        qi = q_ref[:, sl].astype(F32)
        if rope:
            qi = _rope(qi, *tabs)
        qi = (qi * scale).astype(BF16)
        o_t = _attend_t(qi, k_ref.at[:, sl], vt_s, None if ck_ref is None else ck_ref.at[:, sl],
                        cvt_s, (s_a, s_b), m_ref, l_ref, acc_ref, tk)
        if i == 0:
            o1_s[...] = o_t
    o = jnp.transpose(o1_s[...] - lam * o_t)
    o = _rms(o, nw_ref[...]) * (1.0 - lambda_init)
    o_ref[...] = (o * _silu(g_ref[...].astype(F32))).astype(o_ref.dtype)


def _diff_attention(z, keys, key_col0, t_len, layer, lambda_init, diff_lambda, diff_norm_w,
                    cache_k, cache_v, rope):
    m = z.shape[0]
    n_seq = m // t_len
    tq = min(1024, t_len)
    tk = min(512, t_len)
    nq = t_len // tq
    hw = 2 * HEAD_DIM
    cached = cache_k is not None
    in_specs = [
        pl.BlockSpec((tq, hw), lambda b, h, i: (b * nq + i, OFF_FQ // 2 + h)),
        pl.BlockSpec((tq, hw), lambda b, h, i: (b * nq + i, OFF_FG // 2 + h)),
        pl.BlockSpec((t_len, hw), lambda b, h, i: (b, key_col0 + h)),
        pl.BlockSpec((t_len, hw), lambda b, h, i: (b, OFF_FV // 2 + h)),
        pl.BlockSpec((None, 4, HEAD_DIM), lambda b, h, i: (layer, 0, 0)),
        pl.BlockSpec((None, 1, hw), lambda b, h, i: (layer, 0, 0)),
    ]
    args = [z, z, keys, z, diff_lambda, diff_norm_w]
    scratch = [pltpu.VMEM((t_len // tk, hw, tk), BF16)]
    if cached:
        p_len = cache_k.shape[2]
        spec = pl.BlockSpec((None, None, p_len, hw), lambda b, h, i: (b, layer, 0, h))
        in_specs += [spec, spec]
        args += [cache_k, cache_v]
        scratch.append(pltpu.VMEM((hw, p_len), BF16))
    if rope:
        spec = pl.BlockSpec((tq, HEAD_DIM), lambda b, h, i: (i, 0))
        in_specs += [spec, spec, spec]
        args += list(_rope_tables(t_len))
    scratch += [pltpu.VMEM((hw, tq), F32), pltpu.VMEM((tk, tq), F32), pltpu.VMEM((tk, tq), F32),
                pltpu.VMEM((1, tq), F32), pltpu.VMEM((1, tq), F32), pltpu.VMEM((hw, tq), F32)]
    return pl.pallas_call(
        functools.partial(_diff_kernel, rope=rope, cached=cached, tk=tk, lambda_init=lambda_init),
        grid=(n_seq, DIFF_HEADS, nq),
        in_specs=in_specs,
        out_specs=pl.BlockSpec((tq, hw), lambda b, h, i: (b * nq + i, h)),
        out_shape=jax.ShapeDtypeStruct((m, DIFF_W), BF16),
        scratch_shapes=scratch,
        compiler_params=_params("arbitrary", "arbitrary", "arbitrary"),
        name="diff_attention",
    )(*args)


def _split_bf16(a):
    hi = a.astype(BF16)
    return hi, (a - hi.astype(F32)).astype(BF16)


def _mm3(a_parts, b_parts):
    a_hi, a_lo = a_parts
    b_hi, b_lo = b_parts
    return jnp.dot(jnp.concatenate([a_hi, a_hi, a_lo], axis=1), jnp.concatenate([b_hi, b_lo, b_hi], axis=0),
                   preferred_element_type=F32)


def _interleave(*gens):
    gens = list(gens)
    while gens:
        for g in list(gens):
            try:
                next(g)
            except StopIteration:
                gens.remove(g)


def _chunk_cumsum(x, reverse):
    n = x.shape[0]
    row = lax.broadcasted_iota(jnp.int32, x.shape, 0)
    s = 1
    while s < n:
        if reverse:
            x = x + jnp.where(row < n - s, pltpu.roll(x, n - s, 0), 0.0)
        else:
            x = x + jnp.where(row >= s, pltpu.roll(x, s, 0), 0.0)
        s *= 2
    return x


def _dn_kernel(*refs, t_len, has_state):
    refs = list(refs)
    (zq_ref, zk_ref, zv_ref, zg_ref, ab_ref, cwq_ref, cwk_ref, cwv_ref,
     alog_ref, dtb_ref, nw_ref) = refs[:11]
    pos = 11
    s0_ref = None
    if has_state:
        s0_ref = refs[pos]
        pos += 1
    o_ref, sfin_ref = refs[pos:pos + 2]
    raw_s, qn_s, kn_s, vn_s, s_s, o_s = refs[pos + 2:pos + 8]
    n_set = 7
    sets = [refs[pos + 8 + k * n_set:pos + 8 + (k + 1) * n_set] for k in range(3)]
    c_len = DN_CHUNK
    n_chunks = t_len // c_len
    n_groups = n_chunks // PRE_UNROLL
    pad = 8
    blk = min(DN_BLOCK, t_len)
    head = pl.program_id(1)

    def l2n(x):
        return x * lax.rsqrt(jnp.sum(x * x, axis=-1, keepdims=True) + NORM_EPS)

    zero_pad = jnp.zeros((pad, HEAD_DIM), F32)
    for src, cw_ref, dst, post in ((zq_ref, cwq_ref, qn_s, lambda x: l2n(x) * (HEAD_DIM ** -0.5)),
                                   (zk_ref, cwk_ref, kn_s, l2n), (zv_ref, cwv_ref, vn_s, lambda x: x)):
        raw_s[0:pad, :] = zero_pad
        raw_s[pad + t_len:2 * pad + t_len, :] = zero_pad

        def fill(b, carry, src=src):
            r0 = pl.multiple_of(b * blk, blk)
            raw_s[pl.ds(r0 + pad, blk), :] = src[pl.ds(r0, blk), :].astype(F32)
            return carry

        lax.fori_loop(0, t_len // blk, fill, 0)
        cw = cw_ref[...]

        def conv(b, carry, dst=dst, post=post, cw=cw):
            r0 = pl.multiple_of(b * blk, blk)
            n = blk + 2 * pad
            win = raw_s[pl.ds(r0, n), :]
            prev = pltpu.roll(win, 1, 0)[pad:pad + blk]
            nxt = pltpu.roll(win, n - 1, 0)[pad:pad + blk]
            y = _silu(prev * cw[0:1] + win[pad:pad + blk] * cw[1:2] + nxt * cw[2:3])
            dst[pl.ds(r0, blk), :] = post(y)
            return carry

        lax.fori_loop(0, t_len // blk, conv, 0)

    lane = lax.broadcasted_iota(jnp.int32, (c_len, 128), 1)
    row2 = lax.broadcasted_iota(jnp.int32, (2 * c_len, 2 * c_len), 0)
    col2 = lax.broadcasted_iota(jnp.int32, (2 * c_len, 2 * c_len), 1)
    fwd_blk = (row2 < c_len) & (col2 < c_len)
    bwd_blk = (row2 >= c_len) & (col2 >= c_len)
    incl = (fwd_blk & (row2 >= col2)) | (bwd_blk & (row2 <= col2))
    strict = (fwd_blk & (row2 > col2)) | (bwd_blk & (row2 < col2))
    eye = (row2 == col2).astype(F32)
    neg_a = -jnp.exp(alog_ref[...])
    dtb = dtb_ref[...]
    zeros_v = jnp.zeros((c_len, HEAD_DIM), BF16)

    def pick(x, col):
        return jnp.broadcast_to(jnp.sum(jnp.where(lane == col, x, 0.0), axis=-1, keepdims=True),
                                (c_len, HEAD_DIM))

    def rows_of(i):
        return (pl.multiple_of(i * c_len, c_len), pl.multiple_of((n_chunks - 1 - i) * c_len, c_len))

    def pair(ref, i):
        rf, rb = rows_of(i)
        return jnp.concatenate([ref[pl.ds(rf, c_len), :], ref[pl.ds(rb, c_len), :]], axis=0)

    def pre_instance(i, j, bufs):
        u_b, wq_b, qk_b, kdt_b, et_b, rhs_b, l_b = bufs
        rf, rb = rows_of(i)
        q2, k2, v2 = pair(qn_s, i), pair(kn_s, i), pair(vn_s, i)
        gs, betas = [], []
        for d, r0 in enumerate((rf, rb)):
            ab = ab_ref[pl.ds(r0, c_len), :]
            x = ab + dtb
            g_all = neg_a * (jnp.maximum(x, 0.0) + jnp.log1p(jnp.exp(-jnp.abs(x))))
            gs.append(pick(g_all, d * DN_HEADS + head))
            betas.append(pick(jax.nn.sigmoid(ab), (2 + d) * DN_HEADS + head))
        beta2 = jnp.concatenate(betas, axis=0)
        gc2 = jnp.concatenate([_chunk_cumsum(gs[0], False), _chunk_cumsum(gs[1], True)], axis=0)
        tots = [jnp.sum(g, axis=0, keepdims=True) for g in gs]
        tot2 = jnp.concatenate([jnp.broadcast_to(t, (c_len, HEAD_DIM)) for t in tots], axis=0)
        k2b = k2.astype(BF16)
        kk2 = _bdot_nt(k2b, k2b)
        qk2 = _bdot_nt(q2, k2b)
        decay = jnp.where(incl, jnp.exp(jnp.where(incl, gc2 - jnp.transpose(gc2), 0.0)), 0.0)
        e2 = jnp.exp(gc2)
        rhs_b[j] = jnp.concatenate([v2 * beta2, k2 * beta2 * e2], axis=-1).astype(BF16)
        qk_b[j] = (qk2 * decay).astype(BF16)
        wq_b[j, 2 * c_len:4 * c_len, :] = (q2 * e2).astype(BF16)
        kdt_b[j] = jnp.transpose(k2 * jnp.exp(tot2 - gc2)).astype(BF16)
        et_b[j] = jnp.concatenate([jnp.broadcast_to(jnp.exp(t), (8, HEAD_DIM)) for t in tots], axis=1)
        return jnp.where(strict, kk2 * beta2 * decay, 0.0)

    def prep_group(g, bufs):
        l_b = bufs[6]
        for j in range(PRE_UNROLL):
            l_b[j] = pre_instance(PRE_UNROLL * g + j, j, bufs)
            yield

    def solve_group(g, bufs):
        u_b, wq_b, qk_b, kdt_b, et_b, rhs_b, l_b = bufs
        l_mats = [l_b[j] for j in range(PRE_UNROLL)]
        ts = [eye - l for l in l_mats]
        l_parts = [_split_bf16(l) for l in l_mats]
        ps = [_mm3(lp, lp) for lp in l_parts]
        yield
        steps = int(math.log2(DN_CHUNK)) - 1
        for s in range(steps):
            p_parts = [_split_bf16(p) for p in ps]
            ts = [t + _mm3(_split_bf16(t), pp) for t, pp in zip(ts, p_parts)]
            yield
            if s + 1 < steps:
                ps = [_mm3(pp, pp) for pp in p_parts]
                yield
        for j in range(PRE_UNROLL):
            uw = jnp.dot(ts[j].astype(BF16), rhs_b[j], preferred_element_type=F32)
            u_b[j] = uw[:, 0:HEAD_DIM]
            wq_b[j, 0:2 * c_len, :] = uw[:, HEAD_DIM:].astype(BF16)
            yield

    def scan_group(g, bufs):
        u_b, wq_b, qk_b, kdt_b, et_b, rhs_b, l_b = bufs
        for j in range(PRE_UNROLL):
            rf, rb = rows_of(PRE_UNROLL * g + j)
            s2 = s_s[...]
            ws = jnp.dot(wq_b[j], s2.astype(BF16), preferred_element_type=F32)
            yield
            u2 = u_b[j]
            vb_f = (u2[0:c_len] - ws[0:c_len, 0:HEAD_DIM]).astype(BF16)
            vb_b = (u2[c_len:] - ws[c_len:2 * c_len, HEAD_DIM:]).astype(BF16)
            o2 = jnp.dot(qk_b[j], jnp.concatenate([vb_f, vb_b], axis=0), preferred_element_type=F32)
            o_s[0, pl.ds(rf, c_len), :] = ws[2 * c_len:3 * c_len, 0:HEAD_DIM] + o2[0:c_len]
            o_s[1, pl.ds(rb, c_len), :] = ws[3 * c_len:, HEAD_DIM:] + o2[c_len:]
            v_bd = jnp.concatenate([jnp.concatenate([vb_f, zeros_v], axis=1),
                                    jnp.concatenate([zeros_v, vb_b], axis=1)], axis=0)
            s_s[...] = s2 * et_b[j][0:1, :] + jnp.dot(kdt_b[j], v_bd, preferred_element_type=F32)
            yield

    if has_state:
        s_s[...] = jnp.concatenate([s0_ref[0], s0_ref[1]], axis=1).astype(F32)
    else:
        s_s[...] = jnp.zeros(s_s.shape, F32)

    def pipeline_step(h, r):
        gens = []
        for stage, dg in ((prep_group, 2), (solve_group, 1), (scan_group, 0)):
            if isinstance(h, int) and not 0 <= h + dg < n_groups:
                continue
            gens.append(stage(h + dg, sets[(r + dg) % 3]))
        _interleave(*gens)

    for h in range(-2, min(0, n_groups)):
        pipeline_step(h, h % 3)
    n_full = max(n_groups - 2, 0)

    def body(t, carry):
        for r in range(3):
            pipeline_step(3 * t + r, r)
        return carry

    lax.fori_loop(0, n_full // 3, body, 0)
    for h in range(n_full - n_full % 3, n_groups):
        pipeline_step(h, h % 3)
    s_fin = s_s[...]
    sfin_ref[0] = s_fin[:, 0:HEAD_DIM].astype(sfin_ref.dtype)
    sfin_ref[1] = s_fin[:, HEAD_DIM:].astype(sfin_ref.dtype)

    nw = nw_ref[...]

    def finish(b, carry):
        r0 = pl.multiple_of(b * blk, blk)
        o = o_s[0, pl.ds(r0, blk), :] + o_s[1, pl.ds(r0, blk), :]
        o = _rms(o, nw) * _silu(zg_ref[pl.ds(r0, blk), :].astype(F32))
        o_ref[pl.ds(r0, blk), :] = o.astype(o_ref.dtype)
        return carry

    lax.fori_loop(0, t_len // blk, finish, 0)


def _delta_net(z, ab, t_len, layer, conv_w, a_log_row, dt_bias_row, dn_norm_w, state):
    m = z.shape[0]
    n_seq = m // t_len
    n_chunks = t_len // DN_CHUNK
    assert n_chunks % PRE_UNROLL == 0 and t_len % min(DN_BLOCK, t_len) == 0
    has_state = state is not None

    def zspec(off):
        return pl.BlockSpec((t_len, HEAD_DIM), lambda b, h: (b, off + h))

    def cwspec(off):
        return pl.BlockSpec((None, CONV_K, HEAD_DIM), lambda b, h: (layer, 0, off + h))

    row_spec = pl.BlockSpec((None, 1, 128), lambda b, h: (layer, 0, 0))
    in_specs = [zspec(OFF_DQ), zspec(OFF_DK), zspec(OFF_DV), zspec(OFF_DG),
                pl.BlockSpec((t_len, 128), lambda b, h: (b, 0)),
                cwspec(0), cwspec(DN_HEADS), cwspec(2 * DN_HEADS),
                row_spec, row_spec, row_spec]
    args = [z, z, z, z, ab, conv_w, conv_w, conv_w, a_log_row, dt_bias_row, dn_norm_w]
    if has_state:
        in_specs.append(pl.BlockSpec((None, None, 2, None, HEAD_DIM, HEAD_DIM),
                                     lambda b, h: (b, layer, 0, h, 0, 0)))
        args.append(state)
    c2 = 2 * DN_CHUNK
    group_set = [
        pltpu.VMEM((PRE_UNROLL, c2, HEAD_DIM), F32),
        pltpu.VMEM((PRE_UNROLL, 2 * c2, HEAD_DIM), BF16),
        pltpu.VMEM((PRE_UNROLL, c2, c2), BF16),
        pltpu.VMEM((PRE_UNROLL, HEAD_DIM, c2), BF16),
        pltpu.VMEM((PRE_UNROLL, 8, 2 * HEAD_DIM), F32),
        pltpu.VMEM((PRE_UNROLL, c2, 2 * HEAD_DIM), BF16),
        pltpu.VMEM((PRE_UNROLL, c2, c2), F32),
    ]
    scratch = [
        pltpu.VMEM((t_len + 16, HEAD_DIM), F32),
        pltpu.VMEM((t_len, HEAD_DIM), F32),
        pltpu.VMEM((t_len, HEAD_DIM), F32),
        pltpu.VMEM((t_len, HEAD_DIM), F32),
        pltpu.VMEM((HEAD_DIM, 2 * HEAD_DIM), F32),
        pltpu.VMEM((2, t_len, HEAD_DIM), F32),
    ] + group_set * 3
    return pl.pallas_call(
        functools.partial(_dn_kernel, t_len=t_len, has_state=has_state),
        grid=(n_seq, DN_HEADS),
        in_specs=in_specs,
        out_specs=[pl.BlockSpec((t_len, HEAD_DIM), lambda b, h: (b, h)),
                   pl.BlockSpec((None, 2, None, HEAD_DIM, HEAD_DIM), lambda b, h: (b, 0, h, 0, 0))],
        out_shape=[jax.ShapeDtypeStruct((m, DN_W), BF16),
                   jax.ShapeDtypeStruct((n_seq, 2, DN_HEADS, HEAD_DIM, HEAD_DIM), F32)],
        scratch_shapes=scratch,
        compiler_params=_params("arbitrary", "arbitrary"),
        name="delta_net",
    )(*args)


def _outproj_kernel(a1_ref, a2_ref, a3_ref, w_ref, o_ref):
    acc = jnp.dot(a1_ref[...], w_ref[0:DN_W, :], preferred_element_type=F32)
    acc += jnp.dot(a2_ref[...], w_ref[DN_W:DN_W + GQA_W, :], preferred_element_type=F32)
    acc += jnp.dot(a3_ref[...], w_ref[DN_W + GQA_W:MIX_W, :], preferred_element_type=F32)
    o_ref[...] = acc.astype(o_ref.dtype)


def _out_projection(dn, gqa, diff, w_out, layer):
    m = dn.shape[0]
    d = w_out.shape[-1]
    tm = min(1024, m)
    tn = min(1024, d)
    return pl.pallas_call(
        _outproj_kernel,
        grid=(m // tm, d // tn),
        in_specs=[pl.BlockSpec((tm, DN_W), lambda i, j: (i, 0)),
                  pl.BlockSpec((tm, GQA_W), lambda i, j: (i, 0)),
                  pl.BlockSpec((tm, DIFF_W), lambda i, j: (i, 0)),
                  pl.BlockSpec((None, MIX_W, tn), lambda i, j: (layer, 0, j))],
        out_specs=pl.BlockSpec((tm, tn), lambda i, j: (i, j)),
        out_shape=jax.ShapeDtypeStruct((m, d), BF16),
        compiler_params=_params("arbitrary", "arbitrary"),
        name="out_projection",
    )(dn, gqa, diff, w_out)


def _residual_kernel(x_ref, mix_ref, g_ref, pw_ref, o_ref):
    o_ref[...] = x_ref[...] + g_ref[...] * _rms(mix_ref[...].astype(F32), pw_ref[...])


def _post_residual(x, mix, mod, mod_row0, rows_per_mod, layer, post_w):
    m, d = x.shape
    tm = min(256, rows_per_mod)
    tiles_per_mod = rows_per_mod // tm
    base = layer * MOD_ROWS + mod_row0
    return pl.pallas_call(
        _residual_kernel,
        grid=(m // tm,),
        in_specs=[pl.BlockSpec((tm, d), lambda i: (i, 0)),
                  pl.BlockSpec((tm, d), lambda i: (i, 0)),
                  pl.BlockSpec((None, 1, d), lambda i: (base + i // tiles_per_mod, 0, 2)),
                  pl.BlockSpec((None, 1, d), lambda i: (layer, 0, 0))],
        out_specs=pl.BlockSpec((tm, d), lambda i: (i, 0)),
        out_shape=jax.ShapeDtypeStruct((m, d), F32),
        compiler_params=_params("arbitrary"),
        name="post_residual",
    )(x, mix, mod, post_w)


def _mixer_layer(x, t_len, mod, mod_row0, rows_per_mod, layer, lambda_init, wts, caches):
    (pre_w, post_w, w_main, w_ab, w_out, conv_w, a_log_row, dt_bias_row, dn_norm_w,
     q_norm_w, k_norm_w, diff_lambda, diff_norm_w) = wts
    latent = caches is not None
    z, ab = _in_projection(x, mod, mod_row0, rows_per_mod, layer, pre_w, w_main, w_ab)
    if latent:
        cgk, cgv, cfk, cfv, state = caches
        gkeys, fkeys = _key_prep(z, t_len, layer, k_norm_w, rope=True)
        fkey_col0 = 0
    else:
        cgk = cgv = cfk = cfv = state = None
        gkeys = _key_prep(z, t_len, layer, k_norm_w, rope=False)
        fkeys, fkey_col0 = z, OFF_FK // 2
    dn, s_fin = _delta_net(z, ab, t_len, layer, conv_w, a_log_row, dt_bias_row, dn_norm_w, state)
    gqa = _gqa_attention(z, gkeys, t_len, layer, q_norm_w, cgk, cgv, rope=latent)
    diff = _diff_attention(z, fkeys, fkey_col0, t_len, layer, lambda_init, diff_lambda, diff_norm_w,
                           cfk, cfv, rope=latent)
    mix = _out_projection(dn, gqa, diff, w_out, layer)
    y = _post_residual(x, mix, mod, mod_row0, rows_per_mod, layer, post_w)
    return y, z, gkeys, s_fin


def kernel(x_prompt, x_sample, cache_gqa_k, cache_gqa_v, cache_diff_k, cache_diff_v, state_dn, c, c_ctx,
           w_mod, b_mod, pre_norm_w, post_norm_w, w_in, w_out, dn_conv_w, dn_a_log, dn_dt_bias, dn_norm_w,
           gqa_q_norm_w, gqa_k_norm_w, diff_lambda, diff_norm_w):
    n_ctx, t_ctx, d = x_prompt.shape
    n_lat, t_lat, _ = x_sample.shape
    depth = w_in.shape[0]
    p_len = cache_gqa_k.shape[2]
    assert n_lat + 1 <= MOD_ROWS

    n_dn = 4 * DN_W
    w_main = jnp.concatenate([w_in[:, :, :n_dn], w_in[:, :, n_dn + N_GATE_COLS:]], axis=-1).astype(BF16)
    w_ab = jnp.pad(w_in[:, :, n_dn:n_dn + N_GATE_COLS], ((0, 0), (0, 0), (0, 128 - N_GATE_COLS))).astype(BF16)
    w_out_b = w_out.astype(BF16)
    row128 = lambda a: jnp.pad(a.reshape(depth, 1, -1), ((0, 0), (0, 0), (0, 128 - a[0].size)))
    a_log_row = row128(dn_a_log)
    dt_bias_row = row128(dn_dt_bias)
    r3 = lambda a: a.reshape(depth, 1, a.shape[-1])

    cond = jnp.concatenate([c_ctx[None, :], c, jnp.zeros((MOD_ROWS - 1 - n_lat, d), F32)], axis=0)
    mod = _modulation(cond, w_mod, b_mod).reshape(depth * MOD_ROWS, 1, 3 * d)

    cgk = cache_gqa_k.reshape(n_lat, depth, p_len, GQA_KV_W)
    cgv = cache_gqa_v.reshape(n_lat, depth, p_len, GQA_KV_W)
    cfk = cache_diff_k.reshape(n_lat, depth, p_len, DIFF_W)
    cfv = cache_diff_v.reshape(n_lat, depth, p_len, DIFF_W)

    xp = x_prompt.reshape(n_ctx * t_ctx, d)
    xs = x_sample.reshape(n_lat * t_lat, d)
    new_k, new_v, new_dk, new_dv, new_s = [], [], [], [], []
    for l in range(depth):
        lambda_init = 0.8 - 0.6 * math.exp(-0.3 * l)
        wts = (r3(pre_norm_w), r3(post_norm_w), w_main, w_ab, w_out_b, dn_conv_w, a_log_row, dt_bias_row,
               r3(dn_norm_w), r3(gqa_q_norm_w), r3(gqa_k_norm_w), diff_lambda, r3(diff_norm_w))
        xp, z, gk, s_fin = _mixer_layer(xp, t_ctx, mod, 0, n_ctx * t_ctx, l, lambda_init, wts, None)
        zc = lambda off, n: z[:, off * HEAD_DIM:off * HEAD_DIM + n].astype(F32)
        new_k.append(gk.reshape(n_ctx, t_ctx, GQA_KV_HEADS, HEAD_DIM))
        new_v.append(zc(OFF_GV, GQA_KV_W).reshape(n_ctx, t_ctx, GQA_KV_HEADS, HEAD_DIM))
        new_dk.append(zc(OFF_FK, DIFF_W).reshape(n_ctx, t_ctx, DIFF_HEADS, 2, HEAD_DIM))
        new_dv.append(zc(OFF_FV, DIFF_W).reshape(n_ctx, t_ctx, DIFF_HEADS, 2 * HEAD_DIM))
        new_s.append(s_fin)
        xs, _, _, _ = _mixer_layer(xs, t_lat, mod, 1, t_lat, l, lambda_init, wts,
                                   (cgk, cgv, cfk, cfv, state_dn))
    return (xp.reshape(n_ctx, t_ctx, d), xs.reshape(n_lat, t_lat, d),
            jnp.stack(new_k, axis=1), jnp.stack(new_v, axis=1), jnp.stack(new_dk, axis=1),
            jnp.stack(new_dv, axis=1), jnp.stack(new_s, axis=1))
```

```python
import functools
import math

import numpy as np
import jax
import jax.numpy as jnp
from jax import lax
from jax.experimental import pallas as pl
from jax.experimental.pallas import tpu as pltpu

F32 = jnp.float32
BF16 = jnp.bfloat16

HEAD_DIM = 128
GRID_W = 64
ROPE_THETA = 10000.0
NORM_EPS = 1e-6
DN_HEADS = 8
DN_CHUNK = 64
CONV_K = 3
GQA_Q_HEADS = 16
GQA_KV_HEADS = 4
GQA_GROUP = GQA_Q_HEADS // GQA_KV_HEADS
DIFF_HEADS = 4
DN_W = DN_HEADS * HEAD_DIM
GQA_W = GQA_Q_HEADS * HEAD_DIM
GQA_KV_W = GQA_KV_HEADS * HEAD_DIM
DIFF_W = DIFF_HEADS * 2 * HEAD_DIM
MIX_W = DN_W + GQA_W + DIFF_W
N_GATE_COLS = 4 * DN_HEADS

OFF_DQ, OFF_DK, OFF_DV, OFF_DG = 0, 8, 16, 24
OFF_GQ, OFF_GK, OFF_GV, OFF_GG = 32, 48, 52, 56
OFF_FQ, OFF_FK, OFF_FV, OFF_FG = 72, 80, 88, 96
Z_BLOCKS = 104
Z_COLS = Z_BLOCKS * HEAD_DIM

VMEM_LIMIT_BYTES = 56 * 1024 * 1024
MOD_ROWS = 16
LOG2E = math.log2(math.e)
PRE_UNROLL = 4
DN_BLOCK = 512


def _params(*sem):
    return pltpu.CompilerParams(dimension_semantics=sem, vmem_limit_bytes=VMEM_LIMIT_BYTES)


def _silu(x):
    return x * jax.nn.sigmoid(x)


def _bdot(a, b):
    return jnp.dot(a.astype(BF16), b.astype(BF16), preferred_element_type=F32)


def _bdot_nt(a, b):
    return lax.dot_general(a.astype(BF16), b.astype(BF16), (((1,), (1,)), ((), ())),
                           preferred_element_type=F32)


def _mod_kernel(c_ref, w_ref, b_ref, o_ref):
    o_ref[...] = _bdot(_silu(c_ref[...]), w_ref[...]) + b_ref[...]


def _modulation(cond, w_mod, b_mod):
    n_layers, d, n = w_mod.shape
    tn = min(512, n)
    return pl.pallas_call(
        _mod_kernel,
        grid=(n_layers, n // tn),
        in_specs=[
            pl.BlockSpec((MOD_ROWS, d), lambda l, j: (0, 0)),
            pl.BlockSpec((None, d, tn), lambda l, j: (l, 0, j)),
            pl.BlockSpec((None, 1, tn), lambda l, j: (l, 0, j)),
        ],
        out_specs=pl.BlockSpec((None, MOD_ROWS, tn), lambda l, j: (l, 0, j)),
        out_shape=jax.ShapeDtypeStruct((n_layers, MOD_ROWS, n), F32),
        compiler_params=_params("arbitrary", "arbitrary"),
        name="modulation",
    )(cond, w_mod, b_mod.reshape(n_layers, 1, n))


def _inproj_kernel(x_ref, sh_ref, sc_ref, pw_ref, w_ref, wab_ref, z_ref, ab_ref, h_scr):
    @pl.when(pl.program_id(1) == 0)
    def _():
        x = x_ref[...]
        y = x * lax.rsqrt(jnp.mean(x * x, axis=-1, keepdims=True) + NORM_EPS) * pw_ref[...]
        h = (y * (1.0 + sc_ref[...]) + sh_ref[...]).astype(BF16)
        h_scr[...] = h
        ab_ref[...] = jnp.dot(h, wab_ref[...], preferred_element_type=F32)

    z_ref[...] = jnp.dot(h_scr[...], w_ref[...], preferred_element_type=F32).astype(z_ref.dtype)


def _in_projection(x, mod, mod_row0, rows_per_mod, layer, pre_w, w_main, w_ab):
    m, d = x.shape
    tm = min(512, rows_per_mod)
    tn = 1024
    assert rows_per_mod % tm == 0 and m % tm == 0 and Z_COLS % tn == 0
    tiles_per_mod = rows_per_mod // tm
    base = layer * MOD_ROWS + mod_row0

    def mod_map(part):
        return lambda i, j: (base + i // tiles_per_mod, 0, part)

    return pl.pallas_call(
        _inproj_kernel,
        grid=(m // tm, Z_COLS // tn),
        in_specs=[
            pl.BlockSpec((tm, d), lambda i, j: (i, 0)),
            pl.BlockSpec((None, 1, d), mod_map(0)),
            pl.BlockSpec((None, 1, d), mod_map(1)),
            pl.BlockSpec((None, 1, d), lambda i, j: (layer, 0, 0)),
            pl.BlockSpec((None, d, tn), lambda i, j: (layer, 0, j)),
            pl.BlockSpec((None, d, 128), lambda i, j: (layer, 0, 0)),
        ],
        out_specs=[
            pl.BlockSpec((tm, tn), lambda i, j: (i, j)),
            pl.BlockSpec((tm, 128), lambda i, j: (i, 0)),
        ],
        out_shape=[jax.ShapeDtypeStruct((m, Z_COLS), BF16), jax.ShapeDtypeStruct((m, 128), F32)],
        scratch_shapes=[pltpu.VMEM((tm, d), BF16)],
        compiler_params=_params("arbitrary", "arbitrary"),
        name="in_projection",
    )(x, mod, mod, pre_w, w_main, w_ab)


def _rope_tables(t_len):
    half = HEAD_DIM // 2
    t = np.arange(t_len)
    pos = np.stack([t // GRID_W, t % GRID_W], axis=-1).astype(np.float64)
    inv_freq = ROPE_THETA ** (-np.arange(0, half, 2, dtype=np.float64) / half)
    ang = pos[..., None] * inv_freq
    ang = np.concatenate([ang, ang], axis=-1).reshape(t_len, HEAD_DIM)
    first = (np.arange(HEAD_DIM) % half) < half // 2
    cos = np.cos(ang)
    sin = np.sin(ang)
    sin_up = np.where(first, -sin, 0.0)
    sin_dn = np.where(first, 0.0, sin)
    return tuple(jnp.asarray(a, F32) for a in (cos, sin_up, sin_dn))


def _rope(x, cos, sin_up, sin_dn):
    q = HEAD_DIM // 4
    return x * cos + pltpu.roll(x, HEAD_DIM - q, 1) * sin_up + pltpu.roll(x, q, 1) * sin_dn


def _rms(x, w):
    return x * lax.rsqrt(jnp.mean(x * x, axis=-1, keepdims=True) + NORM_EPS) * w


def _keyprep_kernel(*refs, rope, n_fk):
    if rope:
        gk_ref, fk_ref, kw_ref, cos_ref, su_ref, sd_ref, gko_ref, fko_ref = refs
        tabs = (cos_ref[...], su_ref[...], sd_ref[...])
    else:
        gk_ref, kw_ref, gko_ref = refs
    kw = kw_ref[...]
    for j in range(GQA_KV_HEADS):
        sl = slice(j * HEAD_DIM, (j + 1) * HEAD_DIM)
        k = _rms(gk_ref[:, sl].astype(F32), kw)
        if rope:
            k = _rope(k, *tabs)
        gko_ref[:, sl] = k.astype(gko_ref.dtype)
    if rope:
        for j in range(n_fk):
            sl = slice(j * HEAD_DIM, (j + 1) * HEAD_DIM)
            fko_ref[:, sl] = _rope(fk_ref[:, sl].astype(F32), *tabs).astype(fko_ref.dtype)


def _key_prep(z, t_len, layer, k_norm_w, rope):
    m = z.shape[0]
    tm = min(256, t_len)
    tiles_per_seq = t_len // tm
    kw_spec = pl.BlockSpec((None, 1, HEAD_DIM), lambda i: (layer, 0, 0))
    gk_spec = pl.BlockSpec((tm, GQA_KV_W), lambda i: (i, OFF_GK * HEAD_DIM // GQA_KV_W))
    if rope:
        tabs = _rope_tables(t_len)
        tab_spec = pl.BlockSpec((tm, HEAD_DIM), lambda i: (i % tiles_per_seq, 0))
        return pl.pallas_call(
            functools.partial(_keyprep_kernel, rope=True, n_fk=2 * DIFF_HEADS),
            grid=(m // tm,),
            in_specs=[gk_spec, pl.BlockSpec((tm, DIFF_W), lambda i: (i, OFF_FK * HEAD_DIM // DIFF_W)),
                      kw_spec, tab_spec, tab_spec, tab_spec],
            out_specs=[pl.BlockSpec((tm, GQA_KV_W), lambda i: (i, 0)),
                       pl.BlockSpec((tm, DIFF_W), lambda i: (i, 0))],
            out_shape=[jax.ShapeDtypeStruct((m, GQA_KV_W), BF16), jax.ShapeDtypeStruct((m, DIFF_W), BF16)],
            compiler_params=_params("arbitrary"),
            name="key_prep_rope",
        )(z, z, k_norm_w, *tabs)
    return pl.pallas_call(
        functools.partial(_keyprep_kernel, rope=False, n_fk=0),
        grid=(m // tm,),
        in_specs=[gk_spec, kw_spec],
        out_specs=pl.BlockSpec((tm, GQA_KV_W), lambda i: (i, 0)),
        out_shape=jax.ShapeDtypeStruct((m, GQA_KV_W), F32),
        compiler_params=_params("arbitrary"),
        name="key_prep",
    )(z, k_norm_w)


def _fill_vt(v_ref, vt_ref, tk):
    def body(c, carry):
        r0 = pl.multiple_of(c * tk, tk)
        vt_ref[c] = jnp.transpose(v_ref[pl.ds(r0, tk), :].astype(F32)).astype(BF16)
        return carry

    lax.fori_loop(0, v_ref.shape[0] // tk, body, 0)


def _attend_t(q, k_ref, vt_ref, ck_ref, cvt_ref, s_refs, m_ref, l_ref, acc_ref, tk,
              first_scores=None, after_loop=None, at_end=None):
    m_ref[...] = jnp.full(m_ref.shape, -jnp.inf, F32)
    l_ref[...] = jnp.zeros(l_ref.shape, F32)
    acc_ref[...] = jnp.zeros(acc_ref.shape, F32)
    s_a, s_b = s_refs
    n_chunks = k_ref.shape[0] // tk

    def scores(k, s_ref):
        s_ref[0:k.shape[0], :] = _bdot_nt(k, q)

    def softmax_pv(s_ref, vt):
        n = vt.shape[1]
        m_prev = m_ref[...]
        m_new = jnp.maximum(m_prev, jnp.max(s_ref[0:n, :], axis=0, keepdims=True))
        alpha = jnp.exp2(m_prev - m_new)
        p = jnp.exp2(s_ref[0:n, :] - m_new)
        l_ref[...] = alpha * l_ref[...] + jnp.sum(p, axis=0, keepdims=True)
        acc_ref[...] = acc_ref[...] * alpha + jnp.dot(vt, p.astype(BF16), preferred_element_type=F32)
        m_ref[...] = m_new

    def k_chunk(c):
        return k_ref[pl.ds(pl.multiple_of(c * tk, tk), tk), :]

    if first_scores is None:
        scores(k_chunk(0), s_a)
    if n_chunks > 1:
        assert n_chunks % 2 == 0

        def body(i, carry):
            scores(k_chunk(2 * i + 1), s_b)
            softmax_pv(s_a, vt_ref[2 * i])
            scores(k_chunk(2 * i + 2), s_a)
            softmax_pv(s_b, vt_ref[2 * i + 1])
            return carry

        first_iter = 0
        if first_scores is not None:
            assert n_chunks >= 4
            scores(k_chunk(1), s_b)
            softmax_pv(first_scores, vt_ref[0])
            scores(k_chunk(2), s_a)
            softmax_pv(s_b, vt_ref[1])
            first_iter = 1
        lax.fori_loop(first_iter, n_chunks // 2 - 1, body, 0)
        if after_loop is not None:
            after_loop()
        scores(k_chunk(n_chunks - 1), s_b)
        softmax_pv(s_a, vt_ref[n_chunks - 2])
        if ck_ref is not None:
            scores(ck_ref[...], s_a)
        softmax_pv(s_b, vt_ref[n_chunks - 1])
        if ck_ref is not None:
            softmax_pv(s_a, cvt_ref[...])
    else:
        if after_loop is not None:
            after_loop()
        softmax_pv(s_a if first_scores is None else first_scores, vt_ref[0])
        if ck_ref is not None:
            scores(ck_ref[...], s_b)
            softmax_pv(s_b, cvt_ref[...])
    if at_end is not None:
        at_end()
    return acc_ref[...] / l_ref[...]


def _gqa_kernel(*refs, rope, cached, tq, tk):
    refs = list(refs)
    q_ref, qn_ref, g_ref, k_ref, v_ref, qw_ref = refs[:6]
    pos = 6
    ck_ref = cv_ref = cvt_s = None
    if cached:
        ck_ref, cv_ref = refs[pos:pos + 2]
        pos += 2
    if rope:
        tab_refs = refs[pos:pos + 6]
        pos += 6
    o_ref, q_scr, qn_scr, vt_s = refs[pos:pos + 4]
    pos += 4
    if cached:
        cvt_s = refs[pos]
        pos += 1
    s_a, s_b, s_c, m_ref, l_ref, acc_ref = refs[pos:]
    first = pl.program_id(2) == 0
    scale = HEAD_DIM ** -0.5 * LOG2E

    def prepare(src_ref, tabs, dst):
        qw = qw_ref[...]
        for j in range(GQA_GROUP):
            qj = _rms(src_ref[:, j * HEAD_DIM:(j + 1) * HEAD_DIM].astype(F32), qw)
            if rope:
                qj = _rope(qj, *(r[...] for r in tabs))
            dst[j * tq:(j + 1) * tq, :] = (qj * scale).astype(BF16)

    def first_chunk_scores(q_src):
        s_c[...] = _bdot_nt(k_ref[0:tk, :], q_src[...])

    @pl.when(first)
    def _():
        _fill_vt(v_ref, vt_s, tk)
        if cached:
            cvt_s[...] = jnp.transpose(cv_ref[...]).astype(BF16)
        prepare(q_ref, tab_refs[0:3] if rope else None, q_scr)
        first_chunk_scores(q_scr)

    @pl.when(jnp.logical_not(first))
    def _():
        q_scr[...] = qn_scr[...]

    o = jnp.transpose(_attend_t(q_scr[...], k_ref, vt_s, ck_ref, cvt_s, (s_a, s_b), m_ref, l_ref, acc_ref, tk,
                                first_scores=s_c,
                                after_loop=lambda: prepare(qn_ref, tab_refs[3:6] if rope else None, qn_scr),
                                at_end=lambda: first_chunk_scores(qn_scr)))
    for j in range(GQA_GROUP):
        sl = slice(j * HEAD_DIM, (j + 1) * HEAD_DIM)
        o_ref[:, sl] = (o[j * tq:(j + 1) * tq, :] * _silu(g_ref[:, sl].astype(F32))).astype(o_ref.dtype)


def _gqa_attention(z, keys, t_len, layer, q_norm_w, cache_k, cache_v, rope):
    m = z.shape[0]
    n_seq = m // t_len
    tq = min(256, t_len)
    tk = min(512, t_len)
    nq = t_len // tq
    cached = cache_k is not None
    qcol = OFF_GQ * HEAD_DIM // (GQA_GROUP * HEAD_DIM)
    gcol = OFF_GG * HEAD_DIM // (GQA_GROUP * HEAD_DIM)
    nxt = lambda i: jnp.minimum(i + 1, nq - 1)
    in_specs = [
        pl.BlockSpec((tq, GQA_GROUP * HEAD_DIM), lambda b, g, i: (b * nq + i, qcol + g)),
        pl.BlockSpec((tq, GQA_GROUP * HEAD_DIM), lambda b, g, i: (b * nq + nxt(i), qcol + g)),
        pl.BlockSpec((tq, GQA_GROUP * HEAD_DIM), lambda b, g, i: (b * nq + i, gcol + g)),
        pl.BlockSpec((t_len, HEAD_DIM), lambda b, g, i: (b, g)),
        pl.BlockSpec((t_len, HEAD_DIM), lambda b, g, i: (b, OFF_GV + g)),
        pl.BlockSpec((None, 1, HEAD_DIM), lambda b, g, i: (layer, 0, 0)),
    ]
    args = [z, z, z, keys, z, q_norm_w]
    rows = GQA_GROUP * tq
    scratch = [pltpu.VMEM((rows, HEAD_DIM), BF16)] * 2 + [pltpu.VMEM((t_len // tk, HEAD_DIM, tk), BF16)]
    if cached:
        p_len = cache_k.shape[2]
        spec = pl.BlockSpec((None, None, p_len, HEAD_DIM), lambda b, g, i: (b, layer, 0, g))
        in_specs += [spec, spec]
        args += [cache_k, cache_v]
        scratch.append(pltpu.VMEM((HEAD_DIM, p_len), BF16))
    if rope:
        spec = pl.BlockSpec((tq, HEAD_DIM), lambda b, g, i: (i, 0))
        spec_n = pl.BlockSpec((tq, HEAD_DIM), lambda b, g, i: (nxt(i), 0))
        in_specs += [spec, spec, spec, spec_n, spec_n, spec_n]
        args += 2 * list(_rope_tables(t_len))
    scratch += [pltpu.VMEM((tk, rows), F32)] * 3 + [
                pltpu.VMEM((1, rows), F32), pltpu.VMEM((1, rows), F32), pltpu.VMEM((HEAD_DIM, rows), F32)]
    return pl.pallas_call(
        functools.partial(_gqa_kernel, rope=rope, cached=cached, tq=tq, tk=tk),
        grid=(n_seq, GQA_KV_HEADS, nq),
        in_specs=in_specs,
        out_specs=pl.BlockSpec((tq, GQA_GROUP * HEAD_DIM), lambda b, g, i: (b * nq + i, g)),
        out_shape=jax.ShapeDtypeStruct((m, GQA_W), BF16),
        scratch_shapes=scratch,
        compiler_params=_params("arbitrary", "arbitrary", "arbitrary"),
        name="gqa_attention",
    )(*args)


def _diff_kernel(*refs, rope, cached, tk, lambda_init):
    refs = list(refs)
    q_ref, g_ref, k_ref, v_ref, lam_ref, nw_ref = refs[:6]
    pos = 6
    ck_ref = cv_ref = cvt_s = None
    if cached:
        ck_ref, cv_ref = refs[pos:pos + 2]
        pos += 2
    if rope:
        tabs = tuple(r[...] for r in refs[pos:pos + 3])
        pos += 3
    o_ref, vt_s = refs[pos:pos + 2]
    pos += 2
    if cached:
        cvt_s = refs[pos]
        pos += 1
    o1_s, s_a, s_b, m_ref, l_ref, acc_ref = refs[pos:]

    @pl.when(pl.program_id(2) == 0)
    def _():
        _fill_vt(v_ref, vt_s, tk)
        if cached:
            cvt_s[...] = jnp.transpose(cv_ref[...]).astype(BF16)

    scale = HEAD_DIM ** -0.5 * LOG2E
    lp = lam_ref[...]
    lam = (jnp.exp(jnp.sum(lp[0:1] * lp[1:2], axis=-1, keepdims=True))
           - jnp.exp(jnp.sum(lp[2:3] * lp[3:4], axis=-1, keepdims=True)) + lambda_init)
    for i in range(2):
        sl = slice(i * HEAD_DIM, (i + 1) * HEAD_DIM)
        qi = q_ref[:, sl].astype(F32)
        if rope:
            qi = _rope(qi, *tabs)
        qi = (qi * scale).astype(BF16)
        o_t = _attend_t(qi, k_ref.at[:, sl], vt_s, None if ck_ref is None else ck_ref.at[:, sl],
                        cvt_s, (s_a, s_b), m_ref, l_ref, acc_ref, tk)
        if i == 0:
            o1_s[...] = o_t
    o = jnp.transpose(o1_s[...] - lam * o_t)
    o = _rms(o, nw_ref[...]) * (1.0 - lambda_init)
    o_ref[...] = (o * _silu(g_ref[...].astype(F32))).astype(o_ref.dtype)


def _diff_attention(z, keys, key_col0, t_len, layer, lambda_init, diff_lambda, diff_norm_w,
                    cache_k, cache_v, rope):
    m = z.shape[0]
    n_seq = m // t_len
    tq = min(1024, t_len)
    tk = min(512, t_len)
    nq = t_len // tq
    hw = 2 * HEAD_DIM
    cached = cache_k is not None
    in_specs = [
        pl.BlockSpec((tq, hw), lambda b, h, i: (b * nq + i, OFF_FQ // 2 + h)),
        pl.BlockSpec((tq, hw), lambda b, h, i: (b * nq + i, OFF_FG // 2 + h)),
        pl.BlockSpec((t_len, hw), lambda b, h, i: (b, key_col0 + h)),
        pl.BlockSpec((t_len, hw), lambda b, h, i: (b, OFF_FV // 2 + h)),
        pl.BlockSpec((None, 4, HEAD_DIM), lambda b, h, i: (layer, 0, 0)),
        pl.BlockSpec((None, 1, hw), lambda b, h, i: (layer, 0, 0)),
    ]
    args = [z, z, keys, z, diff_lambda, diff_norm_w]
    scratch = [pltpu.VMEM((t_len // tk, hw, tk), BF16)]
    if cached:
        p_len = cache_k.shape[2]
        spec = pl.BlockSpec((None, None, p_len, hw), lambda b, h, i: (b, layer, 0, h))
        in_specs += [spec, spec]
        args += [cache_k, cache_v]
        scratch.append(pltpu.VMEM((hw, p_len), BF16))
    if rope:
        spec = pl.BlockSpec((tq, HEAD_DIM), lambda b, h, i: (i, 0))
        in_specs += [spec, spec, spec]
        args += list(_rope_tables(t_len))
    scratch += [pltpu.VMEM((hw, tq), F32), pltpu.VMEM((tk, tq), F32), pltpu.VMEM((tk, tq), F32),
                pltpu.VMEM((1, tq), F32), pltpu.VMEM((1, tq), F32), pltpu.VMEM((hw, tq), F32)]
    return pl.pallas_call(
        functools.partial(_diff_kernel, rope=rope, cached=cached, tk=tk, lambda_init=lambda_init),
        grid=(n_seq, DIFF_HEADS, nq),
        in_specs=in_specs,
        out_specs=pl.BlockSpec((tq, hw), lambda b, h, i: (b * nq + i, h)),
        out_shape=jax.ShapeDtypeStruct((m, DIFF_W), BF16),
        scratch_shapes=scratch,
        compiler_params=_params("arbitrary", "arbitrary", "arbitrary"),
        name="diff_attention",
    )(*args)


def _split_bf16(a):
    hi = a.astype(BF16)
    return hi, (a - hi.astype(F32)).astype(BF16)


def _mm3(a_parts, b_parts):
    a_hi, a_lo = a_parts
    b_hi, b_lo = b_parts
    return jnp.dot(jnp.concatenate([a_hi, a_hi, a_lo], axis=1), jnp.concatenate([b_hi, b_lo, b_hi], axis=0),
                   preferred_element_type=F32)


def _interleave(*gens):
    gens = list(gens)
    while gens:
        for g in list(gens):
            try:
                next(g)
            except StopIteration:
                gens.remove(g)


def _chunk_cumsum(x, reverse):
    n = x.shape[0]
    row = lax.broadcasted_iota(jnp.int32, x.shape, 0)
    s = 1
    while s < n:
        if reverse:
            x = x + jnp.where(row < n - s, pltpu.roll(x, n - s, 0), 0.0)
        else:
            x = x + jnp.where(row >= s, pltpu.roll(x, s, 0), 0.0)
        s *= 2
    return x


def _dn_kernel(*refs, t_len, has_state):
    refs = list(refs)
    (zq_ref, zk_ref, zv_ref, zg_ref, ab_ref, cwq_ref, cwk_ref, cwv_ref,
     alog_ref, dtb_ref, nw_ref) = refs[:11]
    pos = 11
    s0_ref = None
    if has_state:
        s0_ref = refs[pos]
        pos += 1
    o_ref, sfin_ref = refs[pos:pos + 2]
    raw_s, qn_s, kn_s, vn_s, s_s, o_s = refs[pos + 2:pos + 8]
    n_set = 7
    sets = [refs[pos + 8 + k * n_set:pos + 8 + (k + 1) * n_set] for k in range(3)]
    c_len = DN_CHUNK
    n_chunks = t_len // c_len
    n_groups = n_chunks // PRE_UNROLL
    pad = 8
    blk = min(DN_BLOCK, t_len)
    head = pl.program_id(1)

    def l2n(x):
        return x * lax.rsqrt(jnp.sum(x * x, axis=-1, keepdims=True) + NORM_EPS)

    zero_pad = jnp.zeros((pad, HEAD_DIM), F32)
    for src, cw_ref, dst, post in ((zq_ref, cwq_ref, qn_s, lambda x: l2n(x) * (HEAD_DIM ** -0.5)),
                                   (zk_ref, cwk_ref, kn_s, l2n), (zv_ref, cwv_ref, vn_s, lambda x: x)):
        raw_s[0:pad, :] = zero_pad
        raw_s[pad + t_len:2 * pad + t_len, :] = zero_pad

        def fill(b, carry, src=src):
            r0 = pl.multiple_of(b * blk, blk)
            raw_s[pl.ds(r0 + pad, blk), :] = src[pl.ds(r0, blk), :].astype(F32)
            return carry

        lax.fori_loop(0, t_len // blk, fill, 0)
        cw = cw_ref[...]

        def conv(b, carry, dst=dst, post=post, cw=cw):
            r0 = pl.multiple_of(b * blk, blk)
            n = blk + 2 * pad
            win = raw_s[pl.ds(r0, n), :]
            prev = pltpu.roll(win, 1, 0)[pad:pad + blk]
            nxt = pltpu.roll(win, n - 1, 0)[pad:pad + blk]
            y = _silu(prev * cw[0:1] + win[pad:pad + blk] * cw[1:2] + nxt * cw[2:3])
            dst[pl.ds(r0, blk), :] = post(y)
            return carry

        lax.fori_loop(0, t_len // blk, conv, 0)

    lane = lax.broadcasted_iota(jnp.int32, (c_len, 128), 1)
    row2 = lax.broadcasted_iota(jnp.int32, (2 * c_len, 2 * c_len), 0)
    col2 = lax.broadcasted_iota(jnp.int32, (2 * c_len, 2 * c_len), 1)
    fwd_blk = (row2 < c_len) & (col2 < c_len)
    bwd_blk = (row2 >= c_len) & (col2 >= c_len)
    incl = (fwd_blk & (row2 >= col2)) | (bwd_blk & (row2 <= col2))
    strict = (fwd_blk & (row2 > col2)) | (bwd_blk & (row2 < col2))
    eye = (row2 == col2).astype(F32)
    neg_a = -jnp.exp(alog_ref[...])
    dtb = dtb_ref[...]
    zeros_v = jnp.zeros((c_len, HEAD_DIM), BF16)

    def pick(x, col):
        return jnp.broadcast_to(jnp.sum(jnp.where(lane == col, x, 0.0), axis=-1, keepdims=True),
                                (c_len, HEAD_DIM))

    def rows_of(i):
        return (pl.multiple_of(i * c_len, c_len), pl.multiple_of((n_chunks - 1 - i) * c_len, c_len))

    def pair(ref, i):
        rf, rb = rows_of(i)
        return jnp.concatenate([ref[pl.ds(rf, c_len), :], ref[pl.ds(rb, c_len), :]], axis=0)

    def pre_instance(i, j, bufs):
        u_b, wq_b, qk_b, kdt_b, et_b, rhs_b, l_b = bufs
        rf, rb = rows_of(i)
        q2, k2, v2 = pair(qn_s, i), pair(kn_s, i), pair(vn_s, i)
        gs, betas = [], []
        for d, r0 in enumerate((rf, rb)):
            ab = ab_ref[pl.ds(r0, c_len), :]
            x = ab + dtb
            g_all = neg_a * (jnp.maximum(x, 0.0) + jnp.log1p(jnp.exp(-jnp.abs(x))))
            gs.append(pick(g_all, d * DN_HEADS + head))
            betas.append(pick(jax.nn.sigmoid(ab), (2 + d) * DN_HEADS + head))
        beta2 = jnp.concatenate(betas, axis=0)
        gc2 = jnp.concatenate([_chunk_cumsum(gs[0], False), _chunk_cumsum(gs[1], True)], axis=0)
        tots = [jnp.sum(g, axis=0, keepdims=True) for g in gs]
        tot2 = jnp.concatenate([jnp.broadcast_to(t, (c_len, HEAD_DIM)) for t in tots], axis=0)
        k2b = k2.astype(BF16)
        kk2 = _bdot_nt(k2b, k2b)
        qk2 = _bdot_nt(q2, k2b)
        decay = jnp.where(incl, jnp.exp(jnp.where(incl, gc2 - jnp.transpose(gc2), 0.0)), 0.0)
        e2 = jnp.exp(gc2)
        rhs_b[j] = jnp.concatenate([v2 * beta2, k2 * beta2 * e2], axis=-1).astype(BF16)
        qk_b[j] = (qk2 * decay).astype(BF16)
        wq_b[j, 2 * c_len:4 * c_len, :] = (q2 * e2).astype(BF16)
        kdt_b[j] = jnp.transpose(k2 * jnp.exp(tot2 - gc2)).astype(BF16)
        et_b[j] = jnp.concatenate([jnp.broadcast_to(jnp.exp(t), (8, HEAD_DIM)) for t in tots], axis=1)
        return jnp.where(strict, kk2 * beta2 * decay, 0.0)

    def prep_group(g, bufs):
        l_b = bufs[6]
        for j in range(PRE_UNROLL):
            l_b[j] = pre_instance(PRE_UNROLL * g + j, j, bufs)
            yield

    def solve_group(g, bufs):
        u_b, wq_b, qk_b, kdt_b, et_b, rhs_b, l_b = bufs
        l_mats = [l_b[j] for j in range(PRE_UNROLL)]
        ts = [eye - l for l in l_mats]
        l_parts = [_split_bf16(l) for l in l_mats]
        ps = [_mm3(lp, lp) for lp in l_parts]
        yield
        steps = int(math.log2(DN_CHUNK)) - 1
        for s in range(steps):
            p_parts = [_split_bf16(p) for p in ps]
            ts = [t + _mm3(_split_bf16(t), pp) for t, pp in zip(ts, p_parts)]
            yield
            if s + 1 < steps:
                ps = [_mm3(pp, pp) for pp in p_parts]
                yield
        for j in range(PRE_UNROLL):
            uw = jnp.dot(ts[j].astype(BF16), rhs_b[j], preferred_element_type=F32)
            u_b[j] = uw[:, 0:HEAD_DIM]
            wq_b[j, 0:2 * c_len, :] = uw[:, HEAD_DIM:].astype(BF16)
            yield

    def scan_group(g, bufs):
        u_b, wq_b, qk_b, kdt_b, et_b, rhs_b, l_b = bufs
        for j in range(PRE_UNROLL):
            rf, rb = rows_of(PRE_UNROLL * g + j)
            s2 = s_s[...]
            ws = jnp.dot(wq_b[j], s2.astype(BF16), preferred_element_type=F32)
            yield
            u2 = u_b[j]
            vb_f = (u2[0:c_len] - ws[0:c_len, 0:HEAD_DIM]).astype(BF16)
            vb_b = (u2[c_len:] - ws[c_len:2 * c_len, HEAD_DIM:]).astype(BF16)
            o2 = jnp.dot(qk_b[j], jnp.concatenate([vb_f, vb_b], axis=0), preferred_element_type=F32)
            o_s[0, pl.ds(rf, c_len), :] = ws[2 * c_len:3 * c_len, 0:HEAD_DIM] + o2[0:c_len]
            o_s[1, pl.ds(rb, c_len), :] = ws[3 * c_len:, HEAD_DIM:] + o2[c_len:]
            v_bd = jnp.concatenate([jnp.concatenate([vb_f, zeros_v], axis=1),
                                    jnp.concatenate([zeros_v, vb_b], axis=1)], axis=0)
            s_s[...] = s2 * et_b[j][0:1, :] + jnp.dot(kdt_b[j], v_bd, preferred_element_type=F32)
            yield

    if has_state:
        s_s[...] = jnp.concatenate([s0_ref[0], s0_ref[1]], axis=1).astype(F32)
    else:
        s_s[...] = jnp.zeros(s_s.shape, F32)

    def pipeline_step(h, r):
        gens = []
        for stage, dg in ((prep_group, 2), (solve_group, 1), (scan_group, 0)):
            if isinstance(h, int) and not 0 <= h + dg < n_groups:
                continue
            gens.append(stage(h + dg, sets[(r + dg) % 3]))
        _interleave(*gens)

    for h in range(-2, min(0, n_groups)):
        pipeline_step(h, h % 3)
    n_full = max(n_groups - 2, 0)

    def body(t, carry):
        for r in range(3):
            pipeline_step(3 * t + r, r)
        return carry

    lax.fori_loop(0, n_full // 3, body, 0)
    for h in range(n_full - n_full % 3, n_groups):
        pipeline_step(h, h % 3)
    s_fin = s_s[...]
    sfin_ref[0] = s_fin[:, 0:HEAD_DIM].astype(sfin_ref.dtype)
    sfin_ref[1] = s_fin[:, HEAD_DIM:].astype(sfin_ref.dtype)

    nw = nw_ref[...]

    def finish(b, carry):
        r0 = pl.multiple_of(b * blk, blk)
        o = o_s[0, pl.ds(r0, blk), :] + o_s[1, pl.ds(r0, blk), :]
        o = _rms(o, nw) * _silu(zg_ref[pl.ds(r0, blk), :].astype(F32))
        o_ref[pl.ds(r0, blk), :] = o.astype(o_ref.dtype)
        return carry

    lax.fori_loop(0, t_len // blk, finish, 0)


def _delta_net(z, ab, t_len, layer, conv_w, a_log_row, dt_bias_row, dn_norm_w, state):
    m = z.shape[0]
    n_seq = m // t_len
    n_chunks = t_len // DN_CHUNK
    assert n_chunks % PRE_UNROLL == 0 and t_len % min(DN_BLOCK, t_len) == 0
    has_state = state is not None

    def zspec(off):
        return pl.BlockSpec((t_len, HEAD_DIM), lambda b, h: (b, off + h))

    def cwspec(off):
        return pl.BlockSpec((None, CONV_K, HEAD_DIM), lambda b, h: (layer, 0, off + h))

    row_spec = pl.BlockSpec((None, 1, 128), lambda b, h: (layer, 0, 0))
    in_specs = [zspec(OFF_DQ), zspec(OFF_DK), zspec(OFF_DV), zspec(OFF_DG),
                pl.BlockSpec((t_len, 128), lambda b, h: (b, 0)),
                cwspec(0), cwspec(DN_HEADS), cwspec(2 * DN_HEADS),
                row_spec, row_spec, row_spec]
    args = [z, z, z, z, ab, conv_w, conv_w, conv_w, a_log_row, dt_bias_row, dn_norm_w]
    if has_state:
        in_specs.append(pl.BlockSpec((None, None, 2, None, HEAD_DIM, HEAD_DIM),
                                     lambda b, h: (b, layer, 0, h, 0, 0)))
        args.append(state)
    c2 = 2 * DN_CHUNK
    group_set = [
        pltpu.VMEM((PRE_UNROLL, c2, HEAD_DIM), F32),
        pltpu.VMEM((PRE_UNROLL, 2 * c2, HEAD_DIM), BF16),
        pltpu.VMEM((PRE_UNROLL, c2, c2), BF16),
        pltpu.VMEM((PRE_UNROLL, HEAD_DIM, c2), BF16),
        pltpu.VMEM((PRE_UNROLL, 8, 2 * HEAD_DIM), F32),
        pltpu.VMEM((PRE_UNROLL, c2, 2 * HEAD_DIM), BF16),
        pltpu.VMEM((PRE_UNROLL, c2, c2), F32),
    ]
    scratch = [
        pltpu.VMEM((t_len + 16, HEAD_DIM), F32),
        pltpu.VMEM((t_len, HEAD_DIM), F32),
        pltpu.VMEM((t_len, HEAD_DIM), F32),
        pltpu.VMEM((t_len, HEAD_DIM), F32),
        pltpu.VMEM((HEAD_DIM, 2 * HEAD_DIM), F32),
        pltpu.VMEM((2, t_len, HEAD_DIM), F32),
    ] + group_set * 3
    return pl.pallas_call(
        functools.partial(_dn_kernel, t_len=t_len, has_state=has_state),
        grid=(n_seq, DN_HEADS),
        in_specs=in_specs,
        out_specs=[pl.BlockSpec((t_len, HEAD_DIM), lambda b, h: (b, h)),
                   pl.BlockSpec((None, 2, None, HEAD_DIM, HEAD_DIM), lambda b, h: (b, 0, h, 0, 0))],
        out_shape=[jax.ShapeDtypeStruct((m, DN_W), BF16),
                   jax.ShapeDtypeStruct((n_seq, 2, DN_HEADS, HEAD_DIM, HEAD_DIM), F32)],
        scratch_shapes=scratch,
        compiler_params=_params("arbitrary", "arbitrary"),
        name="delta_net",
    )(*args)


def _outproj_kernel(a1_ref, a2_ref, a3_ref, w_ref, o_ref):
    acc = jnp.dot(a1_ref[...], w_ref[0:DN_W, :], preferred_element_type=F32)
    acc += jnp.dot(a2_ref[...], w_ref[DN_W:DN_W + GQA_W, :], preferred_element_type=F32)
    acc += jnp.dot(a3_ref[...], w_ref[DN_W + GQA_W:MIX_W, :], preferred_element_type=F32)
    o_ref[...] = acc.astype(o_ref.dtype)


def _out_projection(dn, gqa, diff, w_out, layer):
    m = dn.shape[0]
    d = w_out.shape[-1]
    tm = min(1024, m)
    tn = min(1024, d)
    return pl.pallas_call(
        _outproj_kernel,
        grid=(m // tm, d // tn),
        in_specs=[pl.BlockSpec((tm, DN_W), lambda i, j: (i, 0)),
                  pl.BlockSpec((tm, GQA_W), lambda i, j: (i, 0)),
                  pl.BlockSpec((tm, DIFF_W), lambda i, j: (i, 0)),
                  pl.BlockSpec((None, MIX_W, tn), lambda i, j: (layer, 0, j))],
        out_specs=pl.BlockSpec((tm, tn), lambda i, j: (i, j)),
        out_shape=jax.ShapeDtypeStruct((m, d), BF16),
        compiler_params=_params("arbitrary", "arbitrary"),
        name="out_projection",
    )(dn, gqa, diff, w_out)


def _residual_kernel(x_ref, mix_ref, g_ref, pw_ref, o_ref):
    o_ref[...] = x_ref[...] + g_ref[...] * _rms(mix_ref[...].astype(F32), pw_ref[...])


def _post_residual(x, mix, mod, mod_row0, rows_per_mod, layer, post_w):
    m, d = x.shape
    tm = min(256, rows_per_mod)
    tiles_per_mod = rows_per_mod // tm
    base = layer * MOD_ROWS + mod_row0
    return pl.pallas_call(
        _residual_kernel,
        grid=(m // tm,),
        in_specs=[pl.BlockSpec((tm, d), lambda i: (i, 0)),
                  pl.BlockSpec((tm, d), lambda i: (i, 0)),
                  pl.BlockSpec((None, 1, d), lambda i: (base + i // tiles_per_mod, 0, 2)),
                  pl.BlockSpec((None, 1, d), lambda i: (layer, 0, 0))],
        out_specs=pl.BlockSpec((tm, d), lambda i: (i, 0)),
        out_shape=jax.ShapeDtypeStruct((m, d), F32),
        compiler_params=_params("arbitrary"),
        name="post_residual",
    )(x, mix, mod, post_w)


def _mixer_layer(x, t_len, mod, mod_row0, rows_per_mod, layer, lambda_init, wts, caches):
    (pre_w, post_w, w_main, w_ab, w_out, conv_w, a_log_row, dt_bias_row, dn_norm_w,
     q_norm_w, k_norm_w, diff_lambda, diff_norm_w) = wts
    latent = caches is not None
    z, ab = _in_projection(x, mod, mod_row0, rows_per_mod, layer, pre_w, w_main, w_ab)
    if latent:
        cgk, cgv, cfk, cfv, state = caches
        gkeys, fkeys = _key_prep(z, t_len, layer, k_norm_w, rope=True)
        fkey_col0 = 0
    else:
        cgk = cgv = cfk = cfv = state = None
        gkeys = _key_prep(z, t_len, layer, k_norm_w, rope=False)
        fkeys, fkey_col0 = z, OFF_FK // 2
    dn, s_fin = _delta_net(z, ab, t_len, layer, conv_w, a_log_row, dt_bias_row, dn_norm_w, state)
    gqa = _gqa_attention(z, gkeys, t_len, layer, q_norm_w, cgk, cgv, rope=latent)
    diff = _diff_attention(z, fkeys, fkey_col0, t_len, layer, lambda_init, diff_lambda, diff_norm_w,
                           cfk, cfv, rope=latent)
    mix = _out_projection(dn, gqa, diff, w_out, layer)
    y = _post_residual(x, mix, mod, mod_row0, rows_per_mod, layer, post_w)
    return y, z, gkeys, s_fin


def kernel(x_prompt, x_sample, cache_gqa_k, cache_gqa_v, cache_diff_k, cache_diff_v, state_dn, c, c_ctx,
           w_mod, b_mod, pre_norm_w, post_norm_w, w_in, w_out, dn_conv_w, dn_a_log, dn_dt_bias, dn_norm_w,
           gqa_q_norm_w, gqa_k_norm_w, diff_lambda, diff_norm_w):
    n_ctx, t_ctx, d = x_prompt.shape
    n_lat, t_lat, _ = x_sample.shape
    depth = w_in.shape[0]
    p_len = cache_gqa_k.shape[2]
    assert n_lat + 1 <= MOD_ROWS

    n_dn = 4 * DN_W
    w_main = jnp.concatenate([w_in[:, :, :n_dn], w_in[:, :, n_dn + N_GATE_COLS:]], axis=-1).astype(BF16)
    w_ab = jnp.pad(w_in[:, :, n_dn:n_dn + N_GATE_COLS], ((0, 0), (0, 0), (0, 128 - N_GATE_COLS))).astype(BF16)
    w_out_b = w_out.astype(BF16)
    row128 = lambda a: jnp.pad(a.reshape(depth, 1, -1), ((0, 0), (0, 0), (0, 128 - a[0].size)))
    a_log_row = row128(dn_a_log)
    dt_bias_row = row128(dn_dt_bias)
    r3 = lambda a: a.reshape(depth, 1, a.shape[-1])

    cond = jnp.concatenate([c_ctx[None, :], c, jnp.zeros((MOD_ROWS - 1 - n_lat, d), F32)], axis=0)
    mod = _modulation(cond, w_mod, b_mod).reshape(depth * MOD_ROWS, 1, 3 * d)

    cgk = cache_gqa_k.reshape(n_lat, depth, p_len, GQA_KV_W)
    cgv = cache_gqa_v.reshape(n_lat, depth, p_len, GQA_KV_W)
    cfk = cache_diff_k.reshape(n_lat, depth, p_len, DIFF_W)
    cfv = cache_diff_v.reshape(n_lat, depth, p_len, DIFF_W)

    xp = x_prompt.reshape(n_ctx * t_ctx, d)
    xs = x_sample.reshape(n_lat * t_lat, d)
    new_k, new_v, new_dk, new_dv, new_s = [], [], [], [], []
    for l in range(depth):
        lambda_init = 0.8 - 0.6 * math.exp(-0.3 * l)
        wts = (r3(pre_norm_w), r3(post_norm_w), w_main, w_ab, w_out_b, dn_conv_w, a_log_row, dt_bias_row,
               r3(dn_norm_w), r3(gqa_q_norm_w), r3(gqa_k_norm_w), diff_lambda, r3(diff_norm_w))
        xp, z, gk, s_fin = _mixer_layer(xp, t_ctx, mod, 0, n_ctx * t_ctx, l, lambda_init, wts, None)
        zc = lambda off, n: z[:, off * HEAD_DIM:off * HEAD_DIM + n].astype(F32)
        new_k.append(gk.reshape(n_ctx, t_ctx, GQA_KV_HEADS, HEAD_DIM))
        new_v.append(zc(OFF_GV, GQA_KV_W).reshape(n_ctx, t_ctx, GQA_KV_HEADS, HEAD_DIM))
        new_dk.append(zc(OFF_FK, DIFF_W).reshape(n_ctx, t_ctx, DIFF_HEADS, 2, HEAD_DIM))
        new_dv.append(zc(OFF_FV, DIFF_W).reshape(n_ctx, t_ctx, DIFF_HEADS, 2 * HEAD_DIM))
        new_s.append(s_fin)
        xs, _, _, _ = _mixer_layer(xs, t_lat, mod, 1, t_lat, l, lambda_init, wts,
                                   (cgk, cgv, cfk, cfv, state_dn))
    return (xp.reshape(n_ctx, t_ctx, d), xs.reshape(n_lat, t_lat, d),
            jnp.stack(new_k, axis=1), jnp.stack(new_v, axis=1), jnp.stack(new_dk, axis=1),
            jnp.stack(new_dv, axis=1), jnp.stack(new_s, axis=1))
```

```python
import functools
import math

import numpy as np
import jax
import jax.numpy as jnp
from jax import lax
from jax.experimental import pallas as pl
from jax.experimental.pallas import tpu as pltpu

F32 = jnp.float32
BF16 = jnp.bfloat16

HEAD_DIM = 128
GRID_W = 64
ROPE_THETA = 10000.0
NORM_EPS = 1e-6
DN_HEADS = 8
DN_CHUNK = 64
CONV_K = 3
GQA_Q_HEADS = 16
GQA_KV_HEADS = 4
GQA_GROUP = GQA_Q_HEADS // GQA_KV_HEADS
DIFF_HEADS = 4
DN_W = DN_HEADS * HEAD_DIM
GQA_W = GQA_Q_HEADS * HEAD_DIM
GQA_KV_W = GQA_KV_HEADS * HEAD_DIM
DIFF_W = DIFF_HEADS * 2 * HEAD_DIM
MIX_W = DN_W + GQA_W + DIFF_W
N_GATE_COLS = 4 * DN_HEADS

OFF_DQ, OFF_DK, OFF_DV, OFF_DG = 0, 8, 16, 24
OFF_GQ, OFF_GK, OFF_GV, OFF_GG = 32, 48, 52, 56
OFF_FQ, OFF_FK, OFF_FV, OFF_FG = 72, 80, 88, 96
Z_BLOCKS = 104
Z_COLS = Z_BLOCKS * HEAD_DIM

VMEM_LIMIT_BYTES = 56 * 1024 * 1024
MOD_ROWS = 16
LOG2E = math.log2(math.e)
SCORE_PAD = 128
PRE_UNROLL = 4
DN_BLOCK = 512


def _params(*sem):
    return pltpu.CompilerParams(dimension_semantics=sem, vmem_limit_bytes=VMEM_LIMIT_BYTES)


def _silu(x):
    return x * jax.nn.sigmoid(x)


def _bdot(a, b):
    return jnp.dot(a.astype(BF16), b.astype(BF16), preferred_element_type=F32)


def _bdot_nt(a, b):
    return lax.dot_general(a.astype(BF16), b.astype(BF16), (((1,), (1,)), ((), ())),
                           preferred_element_type=F32)


def _mod_kernel(c_ref, w_ref, b_ref, o_ref):
    o_ref[...] = _bdot(_silu(c_ref[...]), w_ref[...]) + b_ref[...]


def _modulation(cond, w_mod, b_mod):
    n_layers, d, n = w_mod.shape
    tn = min(512, n)
    return pl.pallas_call(
        _mod_kernel,
        grid=(n_layers, n // tn),
        in_specs=[
            pl.BlockSpec((MOD_ROWS, d), lambda l, j: (0, 0)),
            pl.BlockSpec((None, d, tn), lambda l, j: (l, 0, j)),
            pl.BlockSpec((None, 1, tn), lambda l, j: (l, 0, j)),
        ],
        out_specs=pl.BlockSpec((None, MOD_ROWS, tn), lambda l, j: (l, 0, j)),
        out_shape=jax.ShapeDtypeStruct((n_layers, MOD_ROWS, n), F32),
        compiler_params=_params("arbitrary", "arbitrary"),
        name="modulation",
    )(cond, w_mod, b_mod.reshape(n_layers, 1, n))


def _inproj_kernel(x_ref, sh_ref, sc_ref, pw_ref, w_ref, wab_ref, z_ref, ab_ref, h_scr):
    @pl.when(pl.program_id(1) == 0)
    def _():
        x = x_ref[...]
        y = x * lax.rsqrt(jnp.mean(x * x, axis=-1, keepdims=True) + NORM_EPS) * pw_ref[...]
        h = (y * (1.0 + sc_ref[...]) + sh_ref[...]).astype(BF16)
        h_scr[...] = h
        ab_ref[...] = jnp.dot(h, wab_ref[...], preferred_element_type=F32)

    z_ref[...] = jnp.dot(h_scr[...], w_ref[...], preferred_element_type=F32).astype(z_ref.dtype)


def _in_projection(x, mod, mod_row0, rows_per_mod, layer, pre_w, w_main, w_ab):
    m, d = x.shape
    tm = min(512, rows_per_mod)
    tn = 1024
    assert rows_per_mod % tm == 0 and m % tm == 0 and Z_COLS % tn == 0
    tiles_per_mod = rows_per_mod // tm
    base = layer * MOD_ROWS + mod_row0

    def mod_map(part):
        return lambda i, j: (base + i // tiles_per_mod, 0, part)

    return pl.pallas_call(
        _inproj_kernel,
        grid=(m // tm, Z_COLS // tn),
        in_specs=[
            pl.BlockSpec((tm, d), lambda i, j: (i, 0)),
            pl.BlockSpec((None, 1, d), mod_map(0)),
            pl.BlockSpec((None, 1, d), mod_map(1)),
            pl.BlockSpec((None, 1, d), lambda i, j: (layer, 0, 0)),
            pl.BlockSpec((None, d, tn), lambda i, j: (layer, 0, j)),
            pl.BlockSpec((None, d, 128), lambda i, j: (layer, 0, 0)),
        ],
        out_specs=[
            pl.BlockSpec((tm, tn), lambda i, j: (i, j)),
            pl.BlockSpec((tm, 128), lambda i, j: (i, 0)),
        ],
        out_shape=[jax.ShapeDtypeStruct((m, Z_COLS), BF16), jax.ShapeDtypeStruct((m, 128), F32)],
        scratch_shapes=[pltpu.VMEM((tm, d), BF16)],
        compiler_params=_params("arbitrary", "arbitrary"),
        name="in_projection",
    )(x, mod, mod, pre_w, w_main, w_ab)


def _rope_tables(t_len):
    half = HEAD_DIM // 2
    t = np.arange(t_len)
    pos = np.stack([t // GRID_W, t % GRID_W], axis=-1).astype(np.float64)
    inv_freq = ROPE_THETA ** (-np.arange(0, half, 2, dtype=np.float64) / half)
    ang = pos[..., None] * inv_freq
    ang = np.concatenate([ang, ang], axis=-1).reshape(t_len, HEAD_DIM)
    first = (np.arange(HEAD_DIM) % half) < half // 2
    cos = np.cos(ang)
    sin = np.sin(ang)
    sin_up = np.where(first, -sin, 0.0)
    sin_dn = np.where(first, 0.0, sin)
    return tuple(jnp.asarray(a, F32) for a in (cos, sin_up, sin_dn))


def _rope(x, cos, sin_up, sin_dn):
    q = HEAD_DIM // 4
    return x * cos + pltpu.roll(x, HEAD_DIM - q, 1) * sin_up + pltpu.roll(x, q, 1) * sin_dn


def _rms(x, w):
    return x * lax.rsqrt(jnp.mean(x * x, axis=-1, keepdims=True) + NORM_EPS) * w


def _keyprep_kernel(*refs, rope, n_fk):
    if rope:
        gk_ref, fk_ref, kw_ref, cos_ref, su_ref, sd_ref, gko_ref, fko_ref = refs
        tabs = (cos_ref[...], su_ref[...], sd_ref[...])
    else:
        gk_ref, kw_ref, gko_ref = refs
    kw = kw_ref[...]
    for j in range(GQA_KV_HEADS):
        sl = slice(j * HEAD_DIM, (j + 1) * HEAD_DIM)
        k = _rms(gk_ref[:, sl].astype(F32), kw)
        if rope:
            k = _rope(k, *tabs)
        gko_ref[:, sl] = k.astype(gko_ref.dtype)
    if rope:
        for j in range(n_fk):
            sl = slice(j * HEAD_DIM, (j + 1) * HEAD_DIM)
            fko_ref[:, sl] = _rope(fk_ref[:, sl].astype(F32), *tabs).astype(fko_ref.dtype)


def _key_prep(z, t_len, layer, k_norm_w, rope):
    m = z.shape[0]
    tm = min(256, t_len)
    tiles_per_seq = t_len // tm
    kw_spec = pl.BlockSpec((None, 1, HEAD_DIM), lambda i: (layer, 0, 0))
    gk_spec = pl.BlockSpec((tm, GQA_KV_W), lambda i: (i, OFF_GK * HEAD_DIM // GQA_KV_W))
    if rope:
        tabs = _rope_tables(t_len)
        tab_spec = pl.BlockSpec((tm, HEAD_DIM), lambda i: (i % tiles_per_seq, 0))
        return pl.pallas_call(
            functools.partial(_keyprep_kernel, rope=True, n_fk=2 * DIFF_HEADS),
            grid=(m // tm,),
            in_specs=[gk_spec, pl.BlockSpec((tm, DIFF_W), lambda i: (i, OFF_FK * HEAD_DIM // DIFF_W)),
                      kw_spec, tab_spec, tab_spec, tab_spec],
            out_specs=[pl.BlockSpec((tm, GQA_KV_W), lambda i: (i, 0)),
                       pl.BlockSpec((tm, DIFF_W), lambda i: (i, 0))],
            out_shape=[jax.ShapeDtypeStruct((m, GQA_KV_W), BF16), jax.ShapeDtypeStruct((m, DIFF_W), BF16)],
            compiler_params=_params("arbitrary"),
            name="key_prep_rope",
        )(z, z, k_norm_w, *tabs)
    return pl.pallas_call(
        functools.partial(_keyprep_kernel, rope=False, n_fk=0),
        grid=(m // tm,),
        in_specs=[gk_spec, kw_spec],
        out_specs=pl.BlockSpec((tm, GQA_KV_W), lambda i: (i, 0)),
        out_shape=jax.ShapeDtypeStruct((m, GQA_KV_W), F32),
        compiler_params=_params("arbitrary"),
        name="key_prep",
    )(z, k_norm_w)


def _fill_vt(v_ref, vt_ref, tk):
    def body(c, carry):
        r0 = pl.multiple_of(c * tk, tk)
        vt_ref[c, :, 0:tk] = jnp.transpose(v_ref[pl.ds(r0, tk), :].astype(F32)).astype(BF16)
        return carry

    lax.fori_loop(0, v_ref.shape[0] // tk, body, 0)


def _attend_t(q, k_ref, vt_ref, ck_ref, cvt_ref, s_refs, m_ref, l_ref, acc_ref, tk,
              first_scores=None, after_loop=None, at_end=None):
    m_ref[...] = jnp.full(m_ref.shape, -jnp.inf, F32)
    l_ref[...] = jnp.zeros(l_ref.shape, F32)
    acc_ref[...] = jnp.zeros(acc_ref.shape, F32)
    s_a, s_b = s_refs
    n_chunks = k_ref.shape[0] // tk
    n_q = q.shape[0]

    def scores(k, s_ref):
        s_ref[0:k.shape[0], 0:n_q] = _bdot_nt(k, q)

    def softmax_pv(s_ref, vt):
        n = vt.shape[1]
        m_prev = m_ref[...]
        m_new = jnp.maximum(m_prev, jnp.max(s_ref[0:n, 0:n_q], axis=0, keepdims=True))
        alpha = jnp.exp2(m_prev - m_new)
        p = jnp.exp2(s_ref[0:n, 0:n_q] - m_new)
        l_ref[...] = alpha * l_ref[...] + jnp.sum(p, axis=0, keepdims=True)
        acc_ref[:, 0:n_q] = acc_ref[:, 0:n_q] * alpha + jnp.dot(vt, p.astype(BF16), preferred_element_type=F32)
        m_ref[...] = m_new

    def k_chunk(c):
        return k_ref[pl.ds(pl.multiple_of(c * tk, tk), tk), :]

    if first_scores is None:
        scores(k_chunk(0), s_a)
    if n_chunks > 1:
        assert n_chunks % 2 == 0

        def body(i, carry):
            scores(k_chunk(2 * i + 1), s_b)
            softmax_pv(s_a, vt_ref[2 * i, :, 0:tk])
            scores(k_chunk(2 * i + 2), s_a)
            softmax_pv(s_b, vt_ref[2 * i + 1, :, 0:tk])
            return carry

        first_iter = 0
        if first_scores is not None:
            assert n_chunks >= 4
            scores(k_chunk(1), s_b)
            softmax_pv(first_scores, vt_ref[0, :, 0:tk])
            scores(k_chunk(2), s_a)
            softmax_pv(s_b, vt_ref[1, :, 0:tk])
            first_iter = 1
        lax.fori_loop(first_iter, n_chunks // 2 - 1, body, 0)
        if after_loop is not None:
            after_loop()
        scores(k_chunk(n_chunks - 1), s_b)
        softmax_pv(s_a, vt_ref[n_chunks - 2, :, 0:tk])
        if ck_ref is not None:
            scores(ck_ref[...], s_a)
        softmax_pv(s_b, vt_ref[n_chunks - 1, :, 0:tk])
        if ck_ref is not None:
            softmax_pv(s_a, cvt_ref[...])
    else:
        if after_loop is not None:
            after_loop()
        softmax_pv(s_a if first_scores is None else first_scores, vt_ref[0, :, 0:tk])
        if ck_ref is not None:
            scores(ck_ref[...], s_b)
            softmax_pv(s_b, cvt_ref[...])
    if at_end is not None:
        at_end()
    return acc_ref[:, 0:n_q] / l_ref[...]


def _gqa_kernel(*refs, rope, cached, tq, tk):
    refs = list(refs)
    q_ref, qn_ref, g_ref, k_ref, v_ref, qw_ref = refs[:6]
    pos = 6
    ck_ref = cv_ref = cvt_s = None
    if cached:
        ck_ref, cv_ref = refs[pos:pos + 2]
        pos += 2
    if rope:
        tab_refs = refs[pos:pos + 6]
        pos += 6
    o_ref, q_scr, qn_scr, vt_s = refs[pos:pos + 4]
    pos += 4
    if cached:
        cvt_s = refs[pos]
        pos += 1
    s_a, s_b, s_c, m_ref, l_ref, acc_ref = refs[pos:]
    first = pl.program_id(2) == 0
    scale = HEAD_DIM ** -0.5 * LOG2E

    def prepare(src_ref, tabs, dst):
        qw = qw_ref[...]
        for j in range(GQA_GROUP):
            qj = _rms(src_ref[:, j * HEAD_DIM:(j + 1) * HEAD_DIM].astype(F32), qw)
            if rope:
                qj = _rope(qj, *(r[...] for r in tabs))
            dst[j * tq:(j + 1) * tq, :] = (qj * scale).astype(BF16)

    def first_chunk_scores(q_src):
        s_c[:, 0:GQA_GROUP * tq] = _bdot_nt(k_ref[0:tk, :], q_src[...])

    @pl.when(first)
    def _():
        _fill_vt(v_ref, vt_s, tk)
        if cached:
            cvt_s[...] = jnp.transpose(cv_ref[...]).astype(BF16)
        prepare(q_ref, tab_refs[0:3] if rope else None, q_scr)
        first_chunk_scores(q_scr)

    @pl.when(jnp.logical_not(first))
    def _():
        q_scr[...] = qn_scr[...]

    o = jnp.transpose(_attend_t(q_scr[...], k_ref, vt_s, ck_ref, cvt_s, (s_a, s_b), m_ref, l_ref, acc_ref, tk,
                                first_scores=s_c,
                                after_loop=lambda: prepare(qn_ref, tab_refs[3:6] if rope else None, qn_scr),
                                at_end=lambda: first_chunk_scores(qn_scr)))
    for j in range(GQA_GROUP):
        sl = slice(j * HEAD_DIM, (j + 1) * HEAD_DIM)
        o_ref[:, sl] = (o[j * tq:(j + 1) * tq, :] * _silu(g_ref[:, sl].astype(F32))).astype(o_ref.dtype)


def _gqa_attention(z, keys, t_len, layer, q_norm_w, cache_k, cache_v, rope):
    m = z.shape[0]
    n_seq = m // t_len
    tq = min(256, t_len)
    tk = min(512, t_len)
    nq = t_len // tq
    cached = cache_k is not None
    qcol = OFF_GQ * HEAD_DIM // (GQA_GROUP * HEAD_DIM)
    gcol = OFF_GG * HEAD_DIM // (GQA_GROUP * HEAD_DIM)
    nxt = lambda i: jnp.minimum(i + 1, nq - 1)
    in_specs = [
        pl.BlockSpec((tq, GQA_GROUP * HEAD_DIM), lambda b, g, i: (b * nq + i, qcol + g)),
        pl.BlockSpec((tq, GQA_GROUP * HEAD_DIM), lambda b, g, i: (b * nq + nxt(i), qcol + g)),
        pl.BlockSpec((tq, GQA_GROUP * HEAD_DIM), lambda b, g, i: (b * nq + i, gcol + g)),
        pl.BlockSpec((t_len, HEAD_DIM), lambda b, g, i: (b, g)),
        pl.BlockSpec((t_len, HEAD_DIM), lambda b, g, i: (b, OFF_GV + g)),
        pl.BlockSpec((None, 1, HEAD_DIM), lambda b, g, i: (layer, 0, 0)),
    ]
    args = [z, z, z, keys, z, q_norm_w]
    rows = GQA_GROUP * tq
    scratch = [pltpu.VMEM((rows, HEAD_DIM), BF16)] * 2 + [pltpu.VMEM((t_len // tk, HEAD_DIM, tk + SCORE_PAD), BF16)]
    if cached:
        p_len = cache_k.shape[2]
        spec = pl.BlockSpec((None, None, p_len, HEAD_DIM), lambda b, g, i: (b, layer, 0, g))
        in_specs += [spec, spec]
        args += [cache_k, cache_v]
        scratch.append(pltpu.VMEM((HEAD_DIM, p_len), BF16))
    if rope:
        spec = pl.BlockSpec((tq, HEAD_DIM), lambda b, g, i: (i, 0))
        spec_n = pl.BlockSpec((tq, HEAD_DIM), lambda b, g, i: (nxt(i), 0))
        in_specs += [spec, spec, spec, spec_n, spec_n, spec_n]
        args += 2 * list(_rope_tables(t_len))
    scratch += [pltpu.VMEM((tk, rows + SCORE_PAD), F32)] * 3 + [
                pltpu.VMEM((1, rows), F32), pltpu.VMEM((1, rows), F32), pltpu.VMEM((HEAD_DIM, rows + SCORE_PAD), F32)]
    return pl.pallas_call(
        functools.partial(_gqa_kernel, rope=rope, cached=cached, tq=tq, tk=tk),
        grid=(n_seq, GQA_KV_HEADS, nq),
        in_specs=in_specs,
        out_specs=pl.BlockSpec((tq, GQA_GROUP * HEAD_DIM), lambda b, g, i: (b * nq + i, g)),
        out_shape=jax.ShapeDtypeStruct((m, GQA_W), BF16),
        scratch_shapes=scratch,
        compiler_params=_params("arbitrary", "arbitrary", "arbitrary"),
        name="gqa_attention",
    )(*args)


def _diff_kernel(*refs, rope, cached, tk, lambda_init):
    refs = list(refs)
    q_ref, g_ref, k_ref, v_ref, lam_ref, nw_ref = refs[:6]
    pos = 6
    ck_ref = cv_ref = cvt_s = None
    if cached:
        ck_ref, cv_ref = refs[pos:pos + 2]
        pos += 2
    if rope:
        tabs = tuple(r[...] for r in refs[pos:pos + 3])
        pos += 3
    o_ref, vt_s = refs[pos:pos + 2]
    pos += 2
    if cached:
        cvt_s = refs[pos]
        pos += 1
    o1_s, s_a, s_b, m_ref, l_ref, acc_ref = refs[pos:]

    @pl.when(pl.program_id(2) == 0)
    def _():
        _fill_vt(v_ref, vt_s, tk)
        if cached:
            cvt_s[...] = jnp.transpose(cv_ref[...]).astype(BF16)

    scale = HEAD_DIM ** -0.5 * LOG2E
    lp = lam_ref[...]
    lam = (jnp.exp(jnp.sum(lp[0:1] * lp[1:2], axis=-1, keepdims=True))
           - jnp.exp(jnp.sum(lp[2:3] * lp[3:4], axis=-1, keepdims=True)) + lambda_init)
    for i in range(2):
        sl = slice(i * HEAD_DIM, (i + 1) * HEAD_DIM)
        qi = q_ref[:, sl].astype(F32)
        if rope:
            qi = _rope(qi, *tabs)
        qi = (qi * scale).astype(BF16)
        o_t = _attend_t(qi, k_ref.at[:, sl], vt_s, None if ck_ref is None else ck_ref.at[:, sl],
                        cvt_s, (s_a, s_b), m_ref, l_ref, acc_ref, tk)
        if i == 0:
            o1_s[...] = o_t
    o = jnp.transpose(o1_s[...] - lam * o_t)
    o = _rms(o, nw_ref[...]) * (1.0 - lambda_init)
    o_ref[...] = (o * _silu(g_ref[...].astype(F32))).astype(o_ref.dtype)


def _diff_attention(z, keys, key_col0, t_len, layer, lambda_init, diff_lambda, diff_norm_w,
                    cache_k, cache_v, rope):
    m = z.shape[0]
    n_seq = m // t_len
    tq = min(1024, t_len)
    tk = min(512, t_len)
    nq = t_len // tq
    hw = 2 * HEAD_DIM
    cached = cache_k is not None
    in_specs = [
        pl.BlockSpec((tq, hw), lambda b, h, i: (b * nq + i, OFF_FQ // 2 + h)),
        pl.BlockSpec((tq, hw), lambda b, h, i: (b * nq + i, OFF_FG // 2 + h)),
        pl.BlockSpec((t_len, hw), lambda b, h, i: (b, key_col0 + h)),
        pl.BlockSpec((t_len, hw), lambda b, h, i: (b, OFF_FV // 2 + h)),
        pl.BlockSpec((None, 4, HEAD_DIM), lambda b, h, i: (layer, 0, 0)),
        pl.BlockSpec((None, 1, hw), lambda b, h, i: (layer, 0, 0)),
    ]
    args = [z, z, keys, z, diff_lambda, diff_norm_w]
    scratch = [pltpu.VMEM((t_len // tk, hw, tk), BF16)]
    if cached:
        p_len = cache_k.shape[2]
        spec = pl.BlockSpec((None, None, p_len, hw), lambda b, h, i: (b, layer, 0, h))
        in_specs += [spec, spec]
        args += [cache_k, cache_v]
        scratch.append(pltpu.VMEM((hw, p_len), BF16))
    if rope:
        spec = pl.BlockSpec((tq, HEAD_DIM), lambda b, h, i: (i, 0))
        in_specs += [spec, spec, spec]
        args += list(_rope_tables(t_len))
    scratch += [pltpu.VMEM((hw, tq), F32), pltpu.VMEM((tk, tq + SCORE_PAD), F32), pltpu.VMEM((tk, tq + SCORE_PAD), F32),
                pltpu.VMEM((1, tq), F32), pltpu.VMEM((1, tq), F32), pltpu.VMEM((hw, tq), F32)]
    return pl.pallas_call(
        functools.partial(_diff_kernel, rope=rope, cached=cached, tk=tk, lambda_init=lambda_init),
        grid=(n_seq, DIFF_HEADS, nq),
        in_specs=in_specs,
        out_specs=pl.BlockSpec((tq, hw), lambda b, h, i: (b * nq + i, h)),
        out_shape=jax.ShapeDtypeStruct((m, DIFF_W), BF16),
        scratch_shapes=scratch,
        compiler_params=_params("arbitrary", "arbitrary", "arbitrary"),
        name="diff_attention",
    )(*args)


def _split_bf16(a):
    hi = a.astype(BF16)
    return hi, (a - hi.astype(F32)).astype(BF16)


def _mm3(a_parts, b_parts):
    a_hi, a_lo = a_parts
    b_hi, b_lo = b_parts
    return jnp.dot(jnp.concatenate([a_hi, a_hi, a_lo], axis=1), jnp.concatenate([b_hi, b_lo, b_hi], axis=0),
                   preferred_element_type=F32)


def _interleave(*gens):
    gens = list(gens)
    while gens:
        for g in list(gens):
            try:
                next(g)
            except StopIteration:
                gens.remove(g)


def _chunk_cumsum(x, reverse):
    n = x.shape[0]
    row = lax.broadcasted_iota(jnp.int32, x.shape, 0)
    s = 1
    while s < n:
        if reverse:
            x = x + jnp.where(row < n - s, pltpu.roll(x, n - s, 0), 0.0)
        else:
            x = x + jnp.where(row >= s, pltpu.roll(x, s, 0), 0.0)
        s *= 2
    return x


def _dn_kernel(*refs, t_len, has_state):
    refs = list(refs)
    (zq_ref, zk_ref, zv_ref, zg_ref, ab_ref, cwq_ref, cwk_ref, cwv_ref,
     alog_ref, dtb_ref, nw_ref) = refs[:11]
    pos = 11
    s0_ref = None
    if has_state:
        s0_ref = refs[pos]
        pos += 1
    o_ref, sfin_ref = refs[pos:pos + 2]
    raw_s, qn_s, kn_s, vn_s, s_s, o_s = refs[pos + 2:pos + 8]
    n_set = 7
    sets = [refs[pos + 8 + k * n_set:pos + 8 + (k + 1) * n_set] for k in range(3)]
    c_len = DN_CHUNK
    n_chunks = t_len // c_len
    n_groups = n_chunks // PRE_UNROLL
    pad = 8
    blk = min(DN_BLOCK, t_len)
    head = pl.program_id(1)

    def l2n(x):
        return x * lax.rsqrt(jnp.sum(x * x, axis=-1, keepdims=True) + NORM_EPS)

    zero_pad = jnp.zeros((pad, HEAD_DIM), F32)
    for src, cw_ref, dst, post in ((zq_ref, cwq_ref, qn_s, lambda x: l2n(x) * (HEAD_DIM ** -0.5)),
                                   (zk_ref, cwk_ref, kn_s, l2n), (zv_ref, cwv_ref, vn_s, lambda x: x)):
        raw_s[0:pad, :] = zero_pad
        raw_s[pad + t_len:2 * pad + t_len, :] = zero_pad

        def fill(b, carry, src=src):
            r0 = pl.multiple_of(b * blk, blk)
            raw_s[pl.ds(r0 + pad, blk), :] = src[pl.ds(r0, blk), :].astype(F32)
            return carry

        lax.fori_loop(0, t_len // blk, fill, 0)
        cw = cw_ref[...]

        def conv(b, carry, dst=dst, post=post, cw=cw):
            r0 = pl.multiple_of(b * blk, blk)
            n = blk + 2 * pad
            win = raw_s[pl.ds(r0, n), :]
            prev = pltpu.roll(win, 1, 0)[pad:pad + blk]
            nxt = pltpu.roll(win, n - 1, 0)[pad:pad + blk]
            y = _silu(prev * cw[0:1] + win[pad:pad + blk] * cw[1:2] + nxt * cw[2:3])
            dst[pl.ds(r0, blk), :] = post(y)
            return carry

        lax.fori_loop(0, t_len // blk, conv, 0)

    lane = lax.broadcasted_iota(jnp.int32, (c_len, 128), 1)
    row2 = lax.broadcasted_iota(jnp.int32, (2 * c_len, 2 * c_len), 0)
    col2 = lax.broadcasted_iota(jnp.int32, (2 * c_len, 2 * c_len), 1)
    fwd_blk = (row2 < c_len) & (col2 < c_len)
    bwd_blk = (row2 >= c_len) & (col2 >= c_len)
    incl = (fwd_blk & (row2 >= col2)) | (bwd_blk & (row2 <= col2))
    strict = (fwd_blk & (row2 > col2)) | (bwd_blk & (row2 < col2))
    eye = (row2 == col2).astype(F32)
    neg_a = -jnp.exp(alog_ref[...])
    dtb = dtb_ref[...]
    zeros_v = jnp.zeros((c_len, HEAD_DIM), BF16)

    def pick(x, col):
        return jnp.broadcast_to(jnp.sum(jnp.where(lane == col, x, 0.0), axis=-1, keepdims=True),
                                (c_len, HEAD_DIM))

    def rows_of(i):
        return (pl.multiple_of(i * c_len, c_len), pl.multiple_of((n_chunks - 1 - i) * c_len, c_len))

    def pair(ref, i):
        rf, rb = rows_of(i)
        return jnp.concatenate([ref[pl.ds(rf, c_len), :], ref[pl.ds(rb, c_len), :]], axis=0)

    def pre_instance(i, j, bufs):
        u_b, wq_b, qk_b, kdt_b, et_b, rhs_b, l_b = bufs
        rf, rb = rows_of(i)
        q2, k2, v2 = pair(qn_s, i), pair(kn_s, i), pair(vn_s, i)
        gs, betas = [], []
        for d, r0 in enumerate((rf, rb)):
            ab = ab_ref[pl.ds(r0, c_len), :]
            x = ab + dtb
            g_all = neg_a * (jnp.maximum(x, 0.0) + jnp.log1p(jnp.exp(-jnp.abs(x))))
            gs.append(pick(g_all, d * DN_HEADS + head))
            betas.append(pick(jax.nn.sigmoid(ab), (2 + d) * DN_HEADS + head))
        beta2 = jnp.concatenate(betas, axis=0)
        gc2 = jnp.concatenate([_chunk_cumsum(gs[0], False), _chunk_cumsum(gs[1], True)], axis=0)
        tots = [jnp.sum(g, axis=0, keepdims=True) for g in gs]
        tot2 = jnp.concatenate([jnp.broadcast_to(t, (c_len, HEAD_DIM)) for t in tots], axis=0)
        k2b = k2.astype(BF16)
        kk2 = _bdot_nt(k2b, k2b)
        qk2 = _bdot_nt(q2, k2b)
        decay = jnp.where(incl, jnp.exp(jnp.where(incl, gc2 - jnp.transpose(gc2), 0.0)), 0.0)
        e2 = jnp.exp(gc2)
        rhs_b[j] = jnp.concatenate([v2 * beta2, k2 * beta2 * e2], axis=-1).astype(BF16)
        qk_b[j] = (qk2 * decay).astype(BF16)
        wq_b[j, 2 * c_len:4 * c_len, :] = (q2 * e2).astype(BF16)
        kdt_b[j] = jnp.transpose(k2 * jnp.exp(tot2 - gc2)).astype(BF16)
        et_b[j] = jnp.concatenate([jnp.broadcast_to(jnp.exp(t), (8, HEAD_DIM)) for t in tots], axis=1)
        return jnp.where(strict, kk2 * beta2 * decay, 0.0)

    def prep_group(g, bufs):
        l_b = bufs[6]
        for j in range(PRE_UNROLL):
            l_b[j] = pre_instance(PRE_UNROLL * g + j, j, bufs)
            yield

    def solve_group(g, bufs):
        u_b, wq_b, qk_b, kdt_b, et_b, rhs_b, l_b = bufs
        l_mats = [l_b[j] for j in range(PRE_UNROLL)]
        ts = [eye - l for l in l_mats]
        l_parts = [_split_bf16(l) for l in l_mats]
        ps = [_mm3(lp, lp) for lp in l_parts]
        yield
        steps = int(math.log2(DN_CHUNK)) - 1
        for s in range(steps):
            p_parts = [_split_bf16(p) for p in ps]
            ts = [t + _mm3(_split_bf16(t), pp) for t, pp in zip(ts, p_parts)]
            yield
            if s + 1 < steps:
                ps = [_mm3(pp, pp) for pp in p_parts]
                yield
        for j in range(PRE_UNROLL):
            uw = jnp.dot(ts[j].astype(BF16), rhs_b[j], preferred_element_type=F32)
            u_b[j] = uw[:, 0:HEAD_DIM]
            wq_b[j, 0:2 * c_len, :] = uw[:, HEAD_DIM:].astype(BF16)
            yield

    def scan_group(g, bufs):
        u_b, wq_b, qk_b, kdt_b, et_b, rhs_b, l_b = bufs
        for j in range(PRE_UNROLL):
            rf, rb = rows_of(PRE_UNROLL * g + j)
            s2 = s_s[...]
            ws = jnp.dot(wq_b[j], s2.astype(BF16), preferred_element_type=F32)
            yield
            u2 = u_b[j]
            vb_f = (u2[0:c_len] - ws[0:c_len, 0:HEAD_DIM]).astype(BF16)
            vb_b = (u2[c_len:] - ws[c_len:2 * c_len, HEAD_DIM:]).astype(BF16)
            o2 = jnp.dot(qk_b[j], jnp.concatenate([vb_f, vb_b], axis=0), preferred_element_type=F32)
            o_s[0, pl.ds(rf, c_len), :] = ws[2 * c_len:3 * c_len, 0:HEAD_DIM] + o2[0:c_len]
            o_s[1, pl.ds(rb, c_len), :] = ws[3 * c_len:, HEAD_DIM:] + o2[c_len:]
            v_bd = jnp.concatenate([jnp.concatenate([vb_f, zeros_v], axis=1),
                                    jnp.concatenate([zeros_v, vb_b], axis=1)], axis=0)
            s_s[...] = s2 * et_b[j][0:1, :] + jnp.dot(kdt_b[j], v_bd, preferred_element_type=F32)
            yield

    if has_state:
        s_s[...] = jnp.concatenate([s0_ref[0], s0_ref[1]], axis=1).astype(F32)
    else:
        s_s[...] = jnp.zeros(s_s.shape, F32)

    def pipeline_step(h, r):
        gens = []
        for stage, dg in ((prep_group, 2), (solve_group, 1), (scan_group, 0)):
            if isinstance(h, int) and not 0 <= h + dg < n_groups:
                continue
            gens.append(stage(h + dg, sets[(r + dg) % 3]))
        _interleave(*gens)

    for h in range(-2, min(0, n_groups)):
        pipeline_step(h, h % 3)
    n_full = max(n_groups - 2, 0)

    def body(t, carry):
        for r in range(3):
            pipeline_step(3 * t + r, r)
        return carry

    lax.fori_loop(0, n_full // 3, body, 0)
    for h in range(n_full - n_full % 3, n_groups):
        pipeline_step(h, h % 3)
    s_fin = s_s[...]
    sfin_ref[0] = s_fin[:, 0:HEAD_DIM].astype(sfin_ref.dtype)
    sfin_ref[1] = s_fin[:, HEAD_DIM:].astype(sfin_ref.dtype)

    nw = nw_ref[...]

    def finish(b, carry):
        r0 = pl.multiple_of(b * blk, blk)
        o = o_s[0, pl.ds(r0, blk), :] + o_s[1, pl.ds(r0, blk), :]
        o = _rms(o, nw) * _silu(zg_ref[pl.ds(r0, blk), :].astype(F32))
        o_ref[pl.ds(r0, blk), :] = o.astype(o_ref.dtype)
        return carry

    lax.fori_loop(0, t_len // blk, finish, 0)


def _delta_net(z, ab, t_len, layer, conv_w, a_log_row, dt_bias_row, dn_norm_w, state):
    m = z.shape[0]
    n_seq = m // t_len
    n_chunks = t_len // DN_CHUNK
    assert n_chunks % PRE_UNROLL == 0 and t_len % min(DN_BLOCK, t_len) == 0
    has_state = state is not None

    def zspec(off):
        return pl.BlockSpec((t_len, HEAD_DIM), lambda b, h: (b, off + h))

    def cwspec(off):
        return pl.BlockSpec((None, CONV_K, HEAD_DIM), lambda b, h: (layer, 0, off + h))

    row_spec = pl.BlockSpec((None, 1, 128), lambda b, h: (layer, 0, 0))
    in_specs = [zspec(OFF_DQ), zspec(OFF_DK), zspec(OFF_DV), zspec(OFF_DG),
                pl.BlockSpec((t_len, 128), lambda b, h: (b, 0)),
                cwspec(0), cwspec(DN_HEADS), cwspec(2 * DN_HEADS),
                row_spec, row_spec, row_spec]
    args = [z, z, z, z, ab, conv_w, conv_w, conv_w, a_log_row, dt_bias_row, dn_norm_w]
    if has_state:
        in_specs.append(pl.BlockSpec((None, None, 2, None, HEAD_DIM, HEAD_DIM),
                                     lambda b, h: (b, layer, 0, h, 0, 0)))
        args.append(state)
    c2 = 2 * DN_CHUNK
    group_set = [
        pltpu.VMEM((PRE_UNROLL, c2, HEAD_DIM), F32),
        pltpu.VMEM((PRE_UNROLL, 2 * c2, HEAD_DIM), BF16),
        pltpu.VMEM((PRE_UNROLL, c2, c2), BF16),
        pltpu.VMEM((PRE_UNROLL, HEAD_DIM, c2), BF16),
        pltpu.VMEM((PRE_UNROLL, 8, 2 * HEAD_DIM), F32),
        pltpu.VMEM((PRE_UNROLL, c2, 2 * HEAD_DIM), BF16),
        pltpu.VMEM((PRE_UNROLL, c2, c2), F32),
    ]
    scratch = [
        pltpu.VMEM((t_len + 16, HEAD_DIM), F32),
        pltpu.VMEM((t_len, HEAD_DIM), F32),
        pltpu.VMEM((t_len, HEAD_DIM), F32),
        pltpu.VMEM((t_len, HEAD_DIM), F32),
        pltpu.VMEM((HEAD_DIM, 2 * HEAD_DIM), F32),
        pltpu.VMEM((2, t_len, HEAD_DIM), F32),
    ] + group_set * 3
    return pl.pallas_call(
        functools.partial(_dn_kernel, t_len=t_len, has_state=has_state),
        grid=(n_seq, DN_HEADS),
        in_specs=in_specs,
        out_specs=[pl.BlockSpec((t_len, HEAD_DIM), lambda b, h: (b, h)),
                   pl.BlockSpec((None, 2, None, HEAD_DIM, HEAD_DIM), lambda b, h: (b, 0, h, 0, 0))],
        out_shape=[jax.ShapeDtypeStruct((m, DN_W), BF16),
                   jax.ShapeDtypeStruct((n_seq, 2, DN_HEADS, HEAD_DIM, HEAD_DIM), F32)],
        scratch_shapes=scratch,
        compiler_params=_params("arbitrary", "arbitrary"),
        name="delta_net",
    )(*args)


def _outproj_kernel(a1_ref, a2_ref, a3_ref, w_ref, o_ref):
    acc = jnp.dot(a1_ref[...], w_ref[0:DN_W, :], preferred_element_type=F32)
    acc += jnp.dot(a2_ref[...], w_ref[DN_W:DN_W + GQA_W, :], preferred_element_type=F32)
    acc += jnp.dot(a3_ref[...], w_ref[DN_W + GQA_W:MIX_W, :], preferred_element_type=F32)
    o_ref[...] = acc.astype(o_ref.dtype)


def _out_projection(dn, gqa, diff, w_out, layer):
    m = dn.shape[0]
    d = w_out.shape[-1]
    tm = min(1024, m)
    tn = min(1024, d)
    return pl.pallas_call(
        _outproj_kernel,
        grid=(m // tm, d // tn),
        in_specs=[pl.BlockSpec((tm, DN_W), lambda i, j: (i, 0)),
                  pl.BlockSpec((tm, GQA_W), lambda i, j: (i, 0)),
                  pl.BlockSpec((tm, DIFF_W), lambda i, j: (i, 0)),
                  pl.BlockSpec((None, MIX_W, tn), lambda i, j: (layer, 0, j))],
        out_specs=pl.BlockSpec((tm, tn), lambda i, j: (i, j)),
        out_shape=jax.ShapeDtypeStruct((m, d), BF16),
        compiler_params=_params("arbitrary", "arbitrary"),
        name="out_projection",
    )(dn, gqa, diff, w_out)


def _residual_kernel(x_ref, mix_ref, g_ref, pw_ref, o_ref):
    o_ref[...] = x_ref[...] + g_ref[...] * _rms(mix_ref[...].astype(F32), pw_ref[...])


def _post_residual(x, mix, mod, mod_row0, rows_per_mod, layer, post_w):
    m, d = x.shape
    tm = min(256, rows_per_mod)
    tiles_per_mod = rows_per_mod // tm
    base = layer * MOD_ROWS + mod_row0
    return pl.pallas_call(
        _residual_kernel,
        grid=(m // tm,),
        in_specs=[pl.BlockSpec((tm, d), lambda i: (i, 0)),
                  pl.BlockSpec((tm, d), lambda i: (i, 0)),
                  pl.BlockSpec((None, 1, d), lambda i: (base + i // tiles_per_mod, 0, 2)),
                  pl.BlockSpec((None, 1, d), lambda i: (layer, 0, 0))],
        out_specs=pl.BlockSpec((tm, d), lambda i: (i, 0)),
        out_shape=jax.ShapeDtypeStruct((m, d), F32),
        compiler_params=_params("arbitrary"),
        name="post_residual",
    )(x, mix, mod, post_w)


def _mixer_layer(x, t_len, mod, mod_row0, rows_per_mod, layer, lambda_init, wts, caches):
    (pre_w, post_w, w_main, w_ab, w_out, conv_w, a_log_row, dt_bias_row, dn_norm_w,
     q_norm_w, k_norm_w, diff_lambda, diff_norm_w) = wts
    latent = caches is not None
    z, ab = _in_projection(x, mod, mod_row0, rows_per_mod, layer, pre_w, w_main, w_ab)
    if latent:
        cgk, cgv, cfk, cfv, state = caches
        gkeys, fkeys = _key_prep(z, t_len, layer, k_norm_w, rope=True)
        fkey_col0 = 0
    else:
        cgk = cgv = cfk = cfv = state = None
        gkeys = _key_prep(z, t_len, layer, k_norm_w, rope=False)
        fkeys, fkey_col0 = z, OFF_FK // 2
    dn, s_fin = _delta_net(z, ab, t_len, layer, conv_w, a_log_row, dt_bias_row, dn_norm_w, state)
    gqa = _gqa_attention(z, gkeys, t_len, layer, q_norm_w, cgk, cgv, rope=latent)
    diff = _diff_attention(z, fkeys, fkey_col0, t_len, layer, lambda_init, diff_lambda, diff_norm_w,
                           cfk, cfv, rope=latent)
    mix = _out_projection(dn, gqa, diff, w_out, layer)
    y = _post_residual(x, mix, mod, mod_row0, rows_per_mod, layer, post_w)
    return y, z, gkeys, s_fin


def kernel(x_prompt, x_sample, cache_gqa_k, cache_gqa_v, cache_diff_k, cache_diff_v, state_dn, c, c_ctx,
           w_mod, b_mod, pre_norm_w, post_norm_w, w_in, w_out, dn_conv_w, dn_a_log, dn_dt_bias, dn_norm_w,
           gqa_q_norm_w, gqa_k_norm_w, diff_lambda, diff_norm_w):
    n_ctx, t_ctx, d = x_prompt.shape
    n_lat, t_lat, _ = x_sample.shape
    depth = w_in.shape[0]
    p_len = cache_gqa_k.shape[2]
    assert n_lat + 1 <= MOD_ROWS

    n_dn = 4 * DN_W
    w_main = jnp.concatenate([w_in[:, :, :n_dn], w_in[:, :, n_dn + N_GATE_COLS:]], axis=-1).astype(BF16)
    w_ab = jnp.pad(w_in[:, :, n_dn:n_dn + N_GATE_COLS], ((0, 0), (0, 0), (0, 128 - N_GATE_COLS))).astype(BF16)
    w_out_b = w_out.astype(BF16)
    row128 = lambda a: jnp.pad(a.reshape(depth, 1, -1), ((0, 0), (0, 0), (0, 128 - a[0].size)))
    a_log_row = row128(dn_a_log)
    dt_bias_row = row128(dn_dt_bias)
    r3 = lambda a: a.reshape(depth, 1, a.shape[-1])

    cond = jnp.concatenate([c_ctx[None, :], c, jnp.zeros((MOD_ROWS - 1 - n_lat, d), F32)], axis=0)
    mod = _modulation(cond, w_mod, b_mod).reshape(depth * MOD_ROWS, 1, 3 * d)

    cgk = cache_gqa_k.reshape(n_lat, depth, p_len, GQA_KV_W)
    cgv = cache_gqa_v.reshape(n_lat, depth, p_len, GQA_KV_W)
    cfk = cache_diff_k.reshape(n_lat, depth, p_len, DIFF_W)
    cfv = cache_diff_v.reshape(n_lat, depth, p_len, DIFF_W)

    xp = x_prompt.reshape(n_ctx * t_ctx, d)
    xs = x_sample.reshape(n_lat * t_lat, d)
    new_k, new_v, new_dk, new_dv, new_s = [], [], [], [], []
    for l in range(depth):
        lambda_init = 0.8 - 0.6 * math.exp(-0.3 * l)
        wts = (r3(pre_norm_w), r3(post_norm_w), w_main, w_ab, w_out_b, dn_conv_w, a_log_row, dt_bias_row,
               r3(dn_norm_w), r3(gqa_q_norm_w), r3(gqa_k_norm_w), diff_lambda, r3(diff_norm_w))
        xp, z, gk, s_fin = _mixer_layer(xp, t_ctx, mod, 0, n_ctx * t_ctx, l, lambda_init, wts, None)
        zc = lambda off, n: z[:, off * HEAD_DIM:off * HEAD_DIM + n].astype(F32)
        new_k.append(gk.reshape(n_ctx, t_ctx, GQA_KV_HEADS, HEAD_DIM))
        new_v.append(zc(OFF_GV, GQA_KV_W).reshape(n_ctx, t_ctx, GQA_KV_HEADS, HEAD_DIM))
        new_dk.append(zc(OFF_FK, DIFF_W).reshape(n_ctx, t_ctx, DIFF_HEADS, 2, HEAD_DIM))
        new_dv.append(zc(OFF_FV, DIFF_W).reshape(n_ctx, t_ctx, DIFF_HEADS, 2 * HEAD_DIM))
        new_s.append(s_fin)
        xs, _, _, _ = _mixer_layer(xs, t_lat, mod, 1, t_lat, l, lambda_init, wts,
                                   (cgk, cgv, cfk, cfv, state_dn))
    return (xp.reshape(n_ctx, t_ctx, d), xs.reshape(n_lat, t_lat, d),
            jnp.stack(new_k, axis=1), jnp.stack(new_v, axis=1), jnp.stack(new_dk, axis=1),
            jnp.stack(new_dv, axis=1), jnp.stack(new_s, axis=1))
```

```python
import functools
import math

import numpy as np
import jax
import jax.numpy as jnp
from jax import lax
from jax.experimental import pallas as pl
from jax.experimental.pallas import tpu as pltpu

F32 = jnp.float32
BF16 = jnp.bfloat16

HEAD_DIM = 128
GRID_W = 64
ROPE_THETA = 10000.0
NORM_EPS = 1e-6
DN_HEADS = 8
DN_CHUNK = 64
CONV_K = 3
GQA_Q_HEADS = 16
GQA_KV_HEADS = 4
GQA_GROUP = GQA_Q_HEADS // GQA_KV_HEADS
DIFF_HEADS = 4
DN_W = DN_HEADS * HEAD_DIM
GQA_W = GQA_Q_HEADS * HEAD_DIM
GQA_KV_W = GQA_KV_HEADS * HEAD_DIM
DIFF_W = DIFF_HEADS * 2 * HEAD_DIM
MIX_W = DN_W + GQA_W + DIFF_W
N_GATE_COLS = 4 * DN_HEADS

OFF_DQ, OFF_DK, OFF_DV, OFF_DG = 0, 8, 16, 24
OFF_GQ, OFF_GK, OFF_GV, OFF_GG = 32, 48, 52, 56
OFF_FQ, OFF_FK, OFF_FV, OFF_FG = 72, 80, 88, 96
Z_BLOCKS = 104
Z_COLS = Z_BLOCKS * HEAD_DIM

VMEM_LIMIT_BYTES = 56 * 1024 * 1024
MOD_ROWS = 16
LOG2E = math.log2(math.e)
PRE_UNROLL = 4
DN_BLOCK = 512


def _params(*sem):
    return pltpu.CompilerParams(dimension_semantics=sem, vmem_limit_bytes=VMEM_LIMIT_BYTES)


def _silu(x):
    return x * jax.nn.sigmoid(x)


def _bdot(a, b):
    return jnp.dot(a.astype(BF16), b.astype(BF16), preferred_element_type=F32)


def _bdot_nt(a, b):
    return lax.dot_general(a.astype(BF16), b.astype(BF16), (((1,), (1,)), ((), ())),
                           preferred_element_type=F32)


def _mod_kernel(c_ref, w_ref, b_ref, o_ref):
    o_ref[...] = _bdot(_silu(c_ref[...]), w_ref[...]) + b_ref[...]


def _modulation(cond, w_mod, b_mod):
    n_layers, d, n = w_mod.shape
    tn = min(512, n)
    return pl.pallas_call(
        _mod_kernel,
        grid=(n_layers, n // tn),
        in_specs=[
            pl.BlockSpec((MOD_ROWS, d), lambda l, j: (0, 0)),
            pl.BlockSpec((None, d, tn), lambda l, j: (l, 0, j)),
            pl.BlockSpec((None, 1, tn), lambda l, j: (l, 0, j)),
        ],
        out_specs=pl.BlockSpec((None, MOD_ROWS, tn), lambda l, j: (l, 0, j)),
        out_shape=jax.ShapeDtypeStruct((n_layers, MOD_ROWS, n), F32),
        compiler_params=_params("arbitrary", "arbitrary"),
        name="modulation",
    )(cond, w_mod, b_mod.reshape(n_layers, 1, n))


def _inproj_kernel(x_ref, sh_ref, sc_ref, pw_ref, w_ref, wab_ref, z_ref, ab_ref, h_scr):
    @pl.when(pl.program_id(1) == 0)
    def _():
        x = x_ref[...]
        y = x * lax.rsqrt(jnp.mean(x * x, axis=-1, keepdims=True) + NORM_EPS) * pw_ref[...]
        h = (y * (1.0 + sc_ref[...]) + sh_ref[...]).astype(BF16)
        h_scr[...] = h
        ab_ref[...] = jnp.dot(h, wab_ref[...], preferred_element_type=F32)

    z_ref[...] = jnp.dot(h_scr[...], w_ref[...], preferred_element_type=F32).astype(z_ref.dtype)


def _in_projection(x, mod, mod_row0, rows_per_mod, layer, pre_w, w_main, w_ab):
    m, d = x.shape
    tm = min(512, rows_per_mod)
    tn = 1024
    assert rows_per_mod % tm == 0 and m % tm == 0 and Z_COLS % tn == 0
    tiles_per_mod = rows_per_mod // tm
    base = layer * MOD_ROWS + mod_row0

    def mod_map(part):
        return lambda i, j: (base + i // tiles_per_mod, 0, part)

    return pl.pallas_call(
        _inproj_kernel,
        grid=(m // tm, Z_COLS // tn),
        in_specs=[
            pl.BlockSpec((tm, d), lambda i, j: (i, 0)),
            pl.BlockSpec((None, 1, d), mod_map(0)),
            pl.BlockSpec((None, 1, d), mod_map(1)),
            pl.BlockSpec((None, 1, d), lambda i, j: (layer, 0, 0)),
            pl.BlockSpec((None, d, tn), lambda i, j: (layer, 0, j)),
            pl.BlockSpec((None, d, 128), lambda i, j: (layer, 0, 0)),
        ],
        out_specs=[
            pl.BlockSpec((tm, tn), lambda i, j: (i, j)),
            pl.BlockSpec((tm, 128), lambda i, j: (i, 0)),
        ],
        out_shape=[jax.ShapeDtypeStruct((m, Z_COLS), BF16), jax.ShapeDtypeStruct((m, 128), F32)],
        scratch_shapes=[pltpu.VMEM((tm, d), BF16)],
        compiler_params=_params("arbitrary", "arbitrary"),
        name="in_projection",
    )(x, mod, mod, pre_w, w_main, w_ab)


def _rope_tables(t_len):
    half = HEAD_DIM // 2
    t = np.arange(t_len)
    pos = np.stack([t // GRID_W, t % GRID_W], axis=-1).astype(np.float64)
    inv_freq = ROPE_THETA ** (-np.arange(0, half, 2, dtype=np.float64) / half)
    ang = pos[..., None] * inv_freq
    ang = np.concatenate([ang, ang], axis=-1).reshape(t_len, HEAD_DIM)
    first = (np.arange(HEAD_DIM) % half) < half // 2
    cos = np.cos(ang)
    sin = np.sin(ang)
    sin_up = np.where(first, -sin, 0.0)
    sin_dn = np.where(first, 0.0, sin)
    return tuple(jnp.asarray(a, F32) for a in (cos, sin_up, sin_dn))


def _rope(x, cos, sin_up, sin_dn):
    q = HEAD_DIM // 4
    return x * cos + pltpu.roll(x, HEAD_DIM - q, 1) * sin_up + pltpu.roll(x, q, 1) * sin_dn


def _rms(x, w):
    return x * lax.rsqrt(jnp.mean(x * x, axis=-1, keepdims=True) + NORM_EPS) * w


def _keyprep_kernel(*refs, rope, n_fk):
    if rope:
        gk_ref, fk_ref, kw_ref, cos_ref, su_ref, sd_ref, gko_ref, fko_ref = refs
        tabs = (cos_ref[...], su_ref[...], sd_ref[...])
    else:
        gk_ref, kw_ref, gko_ref = refs
    kw = kw_ref[...]
    for j in range(GQA_KV_HEADS):
        sl = slice(j * HEAD_DIM, (j + 1) * HEAD_DIM)
        k = _rms(gk_ref[:, sl].astype(F32), kw)
        if rope:
            k = _rope(k, *tabs)
        gko_ref[:, sl] = k.astype(gko_ref.dtype)
    if rope:
        for j in range(n_fk):
            sl = slice(j * HEAD_DIM, (j + 1) * HEAD_DIM)
            fko_ref[:, sl] = _rope(fk_ref[:, sl].astype(F32), *tabs).astype(fko_ref.dtype)


def _key_prep(z, t_len, layer, k_norm_w, rope):
    m = z.shape[0]
    tm = min(256, t_len)
    tiles_per_seq = t_len // tm
    kw_spec = pl.BlockSpec((None, 1, HEAD_DIM), lambda i: (layer, 0, 0))
    gk_spec = pl.BlockSpec((tm, GQA_KV_W), lambda i: (i, OFF_GK * HEAD_DIM // GQA_KV_W))
    if rope:
        tabs = _rope_tables(t_len)
        tab_spec = pl.BlockSpec((tm, HEAD_DIM), lambda i: (i % tiles_per_seq, 0))
        return pl.pallas_call(
            functools.partial(_keyprep_kernel, rope=True, n_fk=2 * DIFF_HEADS),
            grid=(m // tm,),
            in_specs=[gk_spec, pl.BlockSpec((tm, DIFF_W), lambda i: (i, OFF_FK * HEAD_DIM // DIFF_W)),
                      kw_spec, tab_spec, tab_spec, tab_spec],
            out_specs=[pl.BlockSpec((tm, GQA_KV_W), lambda i: (i, 0)),
                       pl.BlockSpec((tm, DIFF_W), lambda i: (i, 0))],
            out_shape=[jax.ShapeDtypeStruct((m, GQA_KV_W), BF16), jax.ShapeDtypeStruct((m, DIFF_W), BF16)],
            compiler_params=_params("arbitrary"),
            name="key_prep_rope",
        )(z, z, k_norm_w, *tabs)
    return pl.pallas_call(
        functools.partial(_keyprep_kernel, rope=False, n_fk=0),
        grid=(m // tm,),
        in_specs=[gk_spec, kw_spec],
        out_specs=pl.BlockSpec((tm, GQA_KV_W), lambda i: (i, 0)),
        out_shape=jax.ShapeDtypeStruct((m, GQA_KV_W), F32),
        compiler_params=_params("arbitrary"),
        name="key_prep",
    )(z, k_norm_w)


def _fill_vt(v_ref, vt_ref, tk):
    def body(c, carry):
        r0 = pl.multiple_of(c * tk, tk)
        vt_ref[c] = jnp.transpose(v_ref[pl.ds(r0, tk), :].astype(F32)).astype(BF16)
        return carry

    lax.fori_loop(0, v_ref.shape[0] // tk, body, 0)


def _attend_t(q, k_ref, vt_ref, ck_ref, cvt_ref, s_refs, m_ref, l_ref, acc_ref, tk,
              first_scores=None, after_loop=None, at_end=None, q_transposed=False):
    m_ref[...] = jnp.full(m_ref.shape, -jnp.inf, F32)
    l_ref[...] = jnp.zeros(l_ref.shape, F32)
    acc_ref[...] = jnp.zeros(acc_ref.shape, F32)
    s_a, s_b = s_refs
    n_chunks = k_ref.shape[0] // tk

    def scores(k, s_ref):
        if q_transposed:
            s_ref[0:k.shape[0], :] = jnp.dot(k.astype(BF16), q, preferred_element_type=F32)
        else:
            s_ref[0:k.shape[0], :] = _bdot_nt(k, q)

    def softmax_pv(s_ref, vt):
        n = vt.shape[1]
        m_prev = m_ref[...]
        m_new = jnp.maximum(m_prev, jnp.max(s_ref[0:n, :], axis=0, keepdims=True))
        alpha = jnp.exp2(m_prev - m_new)
        p = jnp.exp2(s_ref[0:n, :] - m_new)
        l_ref[...] = alpha * l_ref[...] + jnp.sum(p, axis=0, keepdims=True)
        acc_ref[...] = acc_ref[...] * alpha + jnp.dot(vt, p.astype(BF16), preferred_element_type=F32)
        m_ref[...] = m_new

    def k_chunk(c):
        return k_ref[pl.ds(pl.multiple_of(c * tk, tk), tk), :]

    if first_scores is None:
        scores(k_chunk(0), s_a)
    if n_chunks > 1:
        assert n_chunks % 2 == 0

        def body(i, carry):
            scores(k_chunk(2 * i + 1), s_b)
            softmax_pv(s_a, vt_ref[2 * i])
            scores(k_chunk(2 * i + 2), s_a)
            softmax_pv(s_b, vt_ref[2 * i + 1])
            return carry

        first_iter = 0
        if first_scores is not None:
            assert n_chunks >= 4
            scores(k_chunk(1), s_b)
            softmax_pv(first_scores, vt_ref[0])
            scores(k_chunk(2), s_a)
            softmax_pv(s_b, vt_ref[1])
            first_iter = 1
        lax.fori_loop(first_iter, n_chunks // 2 - 1, body, 0)
        if after_loop is not None:
            after_loop()
        scores(k_chunk(n_chunks - 1), s_b)
        softmax_pv(s_a, vt_ref[n_chunks - 2])
        if ck_ref is not None:
            scores(ck_ref[...], s_a)
        softmax_pv(s_b, vt_ref[n_chunks - 1])
        if ck_ref is not None:
            softmax_pv(s_a, cvt_ref[...])
    else:
        if after_loop is not None:
            after_loop()
        softmax_pv(s_a if first_scores is None else first_scores, vt_ref[0])
        if ck_ref is not None:
            scores(ck_ref[...], s_b)
            softmax_pv(s_b, cvt_ref[...])
    if at_end is not None:
        at_end()
    return acc_ref[...] / l_ref[...]


def _gqa_kernel(*refs, rope, cached, tq, tk):
    refs = list(refs)
    q_ref, qn_ref, g_ref, k_ref, v_ref, qw_ref = refs[:6]
    pos = 6
    ck_ref = cv_ref = cvt_s = None
    if cached:
        ck_ref, cv_ref = refs[pos:pos + 2]
        pos += 2
    if rope:
        tab_refs = refs[pos:pos + 6]
        pos += 6
    o_ref, q_scr, qn_scr, vt_s = refs[pos:pos + 4]
    pos += 4
    if cached:
        cvt_s = refs[pos]
        pos += 1
    s_a, s_b, s_c, m_ref, l_ref, acc_ref = refs[pos:]
    first = pl.program_id(2) == 0
    scale = HEAD_DIM ** -0.5 * LOG2E

    def prepare(src_ref, tabs, dst):
        qw = qw_ref[...]
        for j in range(GQA_GROUP):
            qj = _rms(src_ref[:, j * HEAD_DIM:(j + 1) * HEAD_DIM].astype(F32), qw)
            if rope:
                qj = _rope(qj, *(r[...] for r in tabs))
            dst[:, j * tq:(j + 1) * tq] = jnp.transpose(qj * scale).astype(BF16)

    def first_chunk_scores(q_src):
        s_c[...] = jnp.dot(k_ref[0:tk, :].astype(BF16), q_src[...], preferred_element_type=F32)

    @pl.when(first)
    def _():
        _fill_vt(v_ref, vt_s, tk)
        if cached:
            cvt_s[...] = jnp.transpose(cv_ref[...]).astype(BF16)
        prepare(q_ref, tab_refs[0:3] if rope else None, q_scr)
        first_chunk_scores(q_scr)

    @pl.when(jnp.logical_not(first))
    def _():
        q_scr[...] = qn_scr[...]

    o = jnp.transpose(_attend_t(q_scr[...], k_ref, vt_s, ck_ref, cvt_s, (s_a, s_b), m_ref, l_ref, acc_ref, tk,
                                first_scores=s_c, q_transposed=True,
                                after_loop=lambda: prepare(qn_ref, tab_refs[3:6] if rope else None, qn_scr),
                                at_end=lambda: first_chunk_scores(qn_scr)))
    for j in range(GQA_GROUP):
        sl = slice(j * HEAD_DIM, (j + 1) * HEAD_DIM)
        o_ref[:, sl] = (o[j * tq:(j + 1) * tq, :] * _silu(g_ref[:, sl].astype(F32))).astype(o_ref.dtype)


def _gqa_attention(z, keys, t_len, layer, q_norm_w, cache_k, cache_v, rope):
    m = z.shape[0]
    n_seq = m // t_len
    tq = min(256, t_len)
    tk = min(512, t_len)
    nq = t_len // tq
    cached = cache_k is not None
    qcol = OFF_GQ * HEAD_DIM // (GQA_GROUP * HEAD_DIM)
    gcol = OFF_GG * HEAD_DIM // (GQA_GROUP * HEAD_DIM)
    nxt = lambda i: jnp.minimum(i + 1, nq - 1)
    in_specs = [
        pl.BlockSpec((tq, GQA_GROUP * HEAD_DIM), lambda b, g, i: (b * nq + i, qcol + g)),
        pl.BlockSpec((tq, GQA_GROUP * HEAD_DIM), lambda b, g, i: (b * nq + nxt(i), qcol + g)),
        pl.BlockSpec((tq, GQA_GROUP * HEAD_DIM), lambda b, g, i: (b * nq + i, gcol + g)),
        pl.BlockSpec((t_len, HEAD_DIM), lambda b, g, i: (b, g)),
        pl.BlockSpec((t_len, HEAD_DIM), lambda b, g, i: (b, OFF_GV + g)),
        pl.BlockSpec((None, 1, HEAD_DIM), lambda b, g, i: (layer, 0, 0)),
    ]
    args = [z, z, z, keys, z, q_norm_w]
    rows = GQA_GROUP * tq
    scratch = [pltpu.VMEM((HEAD_DIM, rows), BF16)] * 2 + [pltpu.VMEM((t_len // tk, HEAD_DIM, tk), BF16)]
    if cached:
        p_len = cache_k.shape[2]
        spec = pl.BlockSpec((None, None, p_len, HEAD_DIM), lambda b, g, i: (b, layer, 0, g))
        in_specs += [spec, spec]
        args += [cache_k, cache_v]
        scratch.append(pltpu.VMEM((HEAD_DIM, p_len), BF16))
    if rope:
        spec = pl.BlockSpec((tq, HEAD_DIM), lambda b, g, i: (i, 0))
        spec_n = pl.BlockSpec((tq, HEAD_DIM), lambda b, g, i: (nxt(i), 0))
        in_specs += [spec, spec, spec, spec_n, spec_n, spec_n]
        args += 2 * list(_rope_tables(t_len))
    scratch += [pltpu.VMEM((tk, rows), F32)] * 3 + [
                pltpu.VMEM((1, rows), F32), pltpu.VMEM((1, rows), F32), pltpu.VMEM((HEAD_DIM, rows), F32)]
    return pl.pallas_call(
        functools.partial(_gqa_kernel, rope=rope, cached=cached, tq=tq, tk=tk),
        grid=(n_seq, GQA_KV_HEADS, nq),
        in_specs=in_specs,
        out_specs=pl.BlockSpec((tq, GQA_GROUP * HEAD_DIM), lambda b, g, i: (b * nq + i, g)),
        out_shape=jax.ShapeDtypeStruct((m, GQA_W), BF16),
        scratch_shapes=scratch,
        compiler_params=_params("arbitrary", "arbitrary", "arbitrary"),
        name="gqa_attention",
    )(*args)


def _diff_kernel(*refs, rope, cached, tk, lambda_init):
    refs = list(refs)
    q_ref, g_ref, k_ref, v_ref, lam_ref, nw_ref = refs[:6]
    pos = 6
    ck_ref = cv_ref = cvt_s = None
    if cached:
        ck_ref, cv_ref = refs[pos:pos + 2]
        pos += 2
    if rope:
        tabs = tuple(r[...] for r in refs[pos:pos + 3])
        pos += 3
    o_ref, vt_s = refs[pos:pos + 2]
    pos += 2
    if cached:
        cvt_s = refs[pos]
        pos += 1
    o1_s, s_a, s_b, m_ref, l_ref, acc_ref = refs[pos:]

    @pl.when(pl.program_id(2) == 0)
    def _():
        _fill_vt(v_ref, vt_s, tk)
        if cached:
            cvt_s[...] = jnp.transpose(cv_ref[...]).astype(BF16)

    scale = HEAD_DIM ** -0.5 * LOG2E
    lp = lam_ref[...]
    lam = (jnp.exp(jnp.sum(lp[0:1] * lp[1:2], axis=-1, keepdims=True))
           - jnp.exp(jnp.sum(lp[2:3] * lp[3:4], axis=-1, keepdims=True)) + lambda_init)
    for i in range(2):
        sl = slice(i * HEAD_DIM, (i + 1) * HEAD_DIM)
        qi = q_ref[:, sl].astype(F32)
        if rope:
            qi = _rope(qi, *tabs)
        qi = (qi * scale).astype(BF16)
        o_t = _attend_t(qi, k_ref.at[:, sl], vt_s, None if ck_ref is None else ck_ref.at[:, sl],
                        cvt_s, (s_a, s_b), m_ref, l_ref, acc_ref, tk)
        if i == 0:
            o1_s[...] = o_t
    o = jnp.transpose(o1_s[...] - lam * o_t)
    o = _rms(o, nw_ref[...]) * (1.0 - lambda_init)
    o_ref[...] = (o * _silu(g_ref[...].astype(F32))).astype(o_ref.dtype)


def _diff_attention(z, keys, key_col0, t_len, layer, lambda_init, diff_lambda, diff_norm_w,
                    cache_k, cache_v, rope):
    m = z.shape[0]
    n_seq = m // t_len
    tq = min(1024, t_len)
    tk = min(512, t_len)
    nq = t_len // tq
    hw = 2 * HEAD_DIM
    cached = cache_k is not None
    in_specs = [
        pl.BlockSpec((tq, hw), lambda b, h, i: (b * nq + i, OFF_FQ // 2 + h)),
        pl.BlockSpec((tq, hw), lambda b, h, i: (b * nq + i, OFF_FG // 2 + h)),
        pl.BlockSpec((t_len, hw), lambda b, h, i: (b, key_col0 + h)),
        pl.BlockSpec((t_len, hw), lambda b, h, i: (b, OFF_FV // 2 + h)),
        pl.BlockSpec((None, 4, HEAD_DIM), lambda b, h, i: (layer, 0, 0)),
        pl.BlockSpec((None, 1, hw), lambda b, h, i: (layer, 0, 0)),
    ]
    args = [z, z, keys, z, diff_lambda, diff_norm_w]
    scratch = [pltpu.VMEM((t_len // tk, hw, tk), BF16)]
    if cached:
        p_len = cache_k.shape[2]
        spec = pl.BlockSpec((None, None, p_len, hw), lambda b, h, i: (b, layer, 0, h))
        in_specs += [spec, spec]
        args += [cache_k, cache_v]
        scratch.append(pltpu.VMEM((hw, p_len), BF16))
    if rope:
        spec = pl.BlockSpec((tq, HEAD_DIM), lambda b, h, i: (i, 0))
        in_specs += [spec, spec, spec]
        args += list(_rope_tables(t_len))
    scratch += [pltpu.VMEM((hw, tq), F32), pltpu.VMEM((tk, tq), F32), pltpu.VMEM((tk, tq), F32),
                pltpu.VMEM((1, tq), F32), pltpu.VMEM((1, tq), F32), pltpu.VMEM((hw, tq), F32)]
    return pl.pallas_call(
        functools.partial(_diff_kernel, rope=rope, cached=cached, tk=tk, lambda_init=lambda_init),
        grid=(n_seq, DIFF_HEADS, nq),
        in_specs=in_specs,
        out_specs=pl.BlockSpec((tq, hw), lambda b, h, i: (b * nq + i, h)),
        out_shape=jax.ShapeDtypeStruct((m, DIFF_W), BF16),
        scratch_shapes=scratch,
        compiler_params=_params("arbitrary", "arbitrary", "arbitrary"),
        name="diff_attention",
    )(*args)


def _split_bf16(a):
    hi = a.astype(BF16)
    return hi, (a - hi.astype(F32)).astype(BF16)


def _mm3(a_parts, b_parts):
    a_hi, a_lo = a_parts
    b_hi, b_lo = b_parts
    return jnp.dot(jnp.concatenate([a_hi, a_hi, a_lo], axis=1), jnp.concatenate([b_hi, b_lo, b_hi], axis=0),
                   preferred_element_type=F32)


def _interleave(*gens):
    gens = list(gens)
    while gens:
        for g in list(gens):
            try:
                next(g)
            except StopIteration:
                gens.remove(g)


def _chunk_cumsum(x, reverse):
    n = x.shape[0]
    row = lax.broadcasted_iota(jnp.int32, x.shape, 0)
    s = 1
    while s < n:
        if reverse:
            x = x + jnp.where(row < n - s, pltpu.roll(x, n - s, 0), 0.0)
        else:
            x = x + jnp.where(row >= s, pltpu.roll(x, s, 0), 0.0)
        s *= 2
    return x


def _dn_kernel(*refs, t_len, has_state):
    refs = list(refs)
    (zq_ref, zk_ref, zv_ref, zg_ref, ab_ref, cwq_ref, cwk_ref, cwv_ref,
     alog_ref, dtb_ref, nw_ref) = refs[:11]
    pos = 11
    s0_ref = None
    if has_state:
        s0_ref = refs[pos]
        pos += 1
    o_ref, sfin_ref = refs[pos:pos + 2]
    raw_s, qn_s, kn_s, vn_s, s_s, o_s = refs[pos + 2:pos + 8]
    n_set = 7
    sets = [refs[pos + 8 + k * n_set:pos + 8 + (k + 1) * n_set] for k in range(3)]
    c_len = DN_CHUNK
    n_chunks = t_len // c_len
    n_groups = n_chunks // PRE_UNROLL
    pad = 8
    blk = min(DN_BLOCK, t_len)
    head = pl.program_id(1)

    def l2n(x):
        return x * lax.rsqrt(jnp.sum(x * x, axis=-1, keepdims=True) + NORM_EPS)

    zero_pad = jnp.zeros((pad, HEAD_DIM), F32)
    for src, cw_ref, dst, post in ((zq_ref, cwq_ref, qn_s, lambda x: l2n(x) * (HEAD_DIM ** -0.5)),
                                   (zk_ref, cwk_ref, kn_s, l2n), (zv_ref, cwv_ref, vn_s, lambda x: x)):
        raw_s[0:pad, :] = zero_pad
        raw_s[pad + t_len:2 * pad + t_len, :] = zero_pad

        def fill(b, carry, src=src):
            r0 = pl.multiple_of(b * blk, blk)
            raw_s[pl.ds(r0 + pad, blk), :] = src[pl.ds(r0, blk), :].astype(F32)
            return carry

        lax.fori_loop(0, t_len // blk, fill, 0)
        cw = cw_ref[...]

        def conv(b, carry, dst=dst, post=post, cw=cw):
            r0 = pl.multiple_of(b * blk, blk)
            n = blk + 2 * pad
            win = raw_s[pl.ds(r0, n), :]
            prev = pltpu.roll(win, 1, 0)[pad:pad + blk]
            nxt = pltpu.roll(win, n - 1, 0)[pad:pad + blk]
            y = _silu(prev * cw[0:1] + win[pad:pad + blk] * cw[1:2] + nxt * cw[2:3])
            dst[pl.ds(r0, blk), :] = post(y)
            return carry

        lax.fori_loop(0, t_len // blk, conv, 0)

    lane = lax.broadcasted_iota(jnp.int32, (c_len, 128), 1)
    row2 = lax.broadcasted_iota(jnp.int32, (2 * c_len, 2 * c_len), 0)
    col2 = lax.broadcasted_iota(jnp.int32, (2 * c_len, 2 * c_len), 1)
    fwd_blk = (row2 < c_len) & (col2 < c_len)
    bwd_blk = (row2 >= c_len) & (col2 >= c_len)
    incl = (fwd_blk & (row2 >= col2)) | (bwd_blk & (row2 <= col2))
    strict = (fwd_blk & (row2 > col2)) | (bwd_blk & (row2 < col2))
    eye = (row2 == col2).astype(F32)
    neg_a = -jnp.exp(alog_ref[...])
    dtb = dtb_ref[...]
    zeros_v = jnp.zeros((c_len, HEAD_DIM), BF16)

    def pick(x, col):
        return jnp.broadcast_to(jnp.sum(jnp.where(lane == col, x, 0.0), axis=-1, keepdims=True),
                                (c_len, HEAD_DIM))

    def rows_of(i):
        return (pl.multiple_of(i * c_len, c_len), pl.multiple_of((n_chunks - 1 - i) * c_len, c_len))

    def pair(ref, i):
        rf, rb = rows_of(i)
        return jnp.concatenate([ref[pl.ds(rf, c_len), :], ref[pl.ds(rb, c_len), :]], axis=0)

    def pre_instance(i, j, bufs):
        u_b, wq_b, qk_b, kdt_b, et_b, rhs_b, l_b = bufs
        rf, rb = rows_of(i)
        q2, k2, v2 = pair(qn_s, i), pair(kn_s, i), pair(vn_s, i)
        gs, betas = [], []
        for d, r0 in enumerate((rf, rb)):
            ab = ab_ref[pl.ds(r0, c_len), :]
            x = ab + dtb
            g_all = neg_a * (jnp.maximum(x, 0.0) + jnp.log1p(jnp.exp(-jnp.abs(x))))
            gs.append(pick(g_all, d * DN_HEADS + head))
            betas.append(pick(jax.nn.sigmoid(ab), (2 + d) * DN_HEADS + head))
        beta2 = jnp.concatenate(betas, axis=0)
        gc2 = jnp.concatenate([_chunk_cumsum(gs[0], False), _chunk_cumsum(gs[1], True)], axis=0)
        tots = [jnp.sum(g, axis=0, keepdims=True) for g in gs]
        tot2 = jnp.concatenate([jnp.broadcast_to(t, (c_len, HEAD_DIM)) for t in tots], axis=0)
        k2b = k2.astype(BF16)
        kk2 = _bdot_nt(k2b, k2b)
        qk2 = _bdot_nt(q2, k2b)
        decay = jnp.where(incl, jnp.exp(jnp.where(incl, gc2 - jnp.transpose(gc2), 0.0)), 0.0)
        e2 = jnp.exp(gc2)
        rhs_b[j] = jnp.concatenate([v2 * beta2, k2 * beta2 * e2], axis=-1).astype(BF16)
        qk_b[j] = (qk2 * decay).astype(BF16)
        wq_b[j, 2 * c_len:4 * c_len, :] = (q2 * e2).astype(BF16)
        kdt_b[j] = jnp.transpose(k2 * jnp.exp(tot2 - gc2)).astype(BF16)
        et_b[j] = jnp.concatenate([jnp.broadcast_to(jnp.exp(t), (8, HEAD_DIM)) for t in tots], axis=1)
        return jnp.where(strict, kk2 * beta2 * decay, 0.0)

    def prep_group(g, bufs):
        l_b = bufs[6]
        for j in range(PRE_UNROLL):
            l_b[j] = pre_instance(PRE_UNROLL * g + j, j, bufs)
            yield

    def solve_group(g, bufs):
        u_b, wq_b, qk_b, kdt_b, et_b, rhs_b, l_b = bufs
        l_mats = [l_b[j] for j in range(PRE_UNROLL)]
        ts = [eye - l for l in l_mats]
        l_parts = [_split_bf16(l) for l in l_mats]
        ps = [_mm3(lp, lp) for lp in l_parts]
        yield
        steps = int(math.log2(DN_CHUNK)) - 1
        for s in range(steps):
            p_parts = [_split_bf16(p) for p in ps]
            ts = [t + _mm3(_split_bf16(t), pp) for t, pp in zip(ts, p_parts)]
            yield
            if s + 1 < steps:
                ps = [_mm3(pp, pp) for pp in p_parts]
                yield
        for j in range(PRE_UNROLL):
            uw = jnp.dot(ts[j].astype(BF16), rhs_b[j], preferred_element_type=F32)
            u_b[j] = uw[:, 0:HEAD_DIM]
            wq_b[j, 0:2 * c_len, :] = uw[:, HEAD_DIM:].astype(BF16)
            yield

    def scan_group(g, bufs):
        u_b, wq_b, qk_b, kdt_b, et_b, rhs_b, l_b = bufs
        for j in range(PRE_UNROLL):
            rf, rb = rows_of(PRE_UNROLL * g + j)
            s2 = s_s[...]
            ws = jnp.dot(wq_b[j], s2.astype(BF16), preferred_element_type=F32)
            yield
            u2 = u_b[j]
            vb_f = (u2[0:c_len] - ws[0:c_len, 0:HEAD_DIM]).astype(BF16)
            vb_b = (u2[c_len:] - ws[c_len:2 * c_len, HEAD_DIM:]).astype(BF16)
            o2 = jnp.dot(qk_b[j], jnp.concatenate([vb_f, vb_b], axis=0), preferred_element_type=F32)
            o_s[0, pl.ds(rf, c_len), :] = ws[2 * c_len:3 * c_len, 0:HEAD_DIM] + o2[0:c_len]
            o_s[1, pl.ds(rb, c_len), :] = ws[3 * c_len:, HEAD_DIM:] + o2[c_len:]
            v_bd = jnp.concatenate([jnp.concatenate([vb_f, zeros_v], axis=1),
                                    jnp.concatenate([zeros_v, vb_b], axis=1)], axis=0)
            s_s[...] = s2 * et_b[j][0:1, :] + jnp.dot(kdt_b[j], v_bd, preferred_element_type=F32)
            yield

    if has_state:
        s_s[...] = jnp.concatenate([s0_ref[0], s0_ref[1]], axis=1).astype(F32)
    else:
        s_s[...] = jnp.zeros(s_s.shape, F32)

    def pipeline_step(h, r):
        gens = []
        for stage, dg in ((prep_group, 2), (solve_group, 1), (scan_group, 0)):
            if isinstance(h, int) and not 0 <= h + dg < n_groups:
                continue
            gens.append(stage(h + dg, sets[(r + dg) % 3]))
        _interleave(*gens)

    for h in range(-2, min(0, n_groups)):
        pipeline_step(h, h % 3)
    n_full = max(n_groups - 2, 0)

    def body(t, carry):
        for r in range(3):
            pipeline_step(3 * t + r, r)
        return carry

    lax.fori_loop(0, n_full // 3, body, 0)
    for h in range(n_full - n_full % 3, n_groups):
        pipeline_step(h, h % 3)
    s_fin = s_s[...]
    sfin_ref[0] = s_fin[:, 0:HEAD_DIM].astype(sfin_ref.dtype)
    sfin_ref[1] = s_fin[:, HEAD_DIM:].astype(sfin_ref.dtype)

    nw = nw_ref[...]

    def finish(b, carry):
        r0 = pl.multiple_of(b * blk, blk)
        o = o_s[0, pl.ds(r0, blk), :] + o_s[1, pl.ds(r0, blk), :]
        o = _rms(o, nw) * _silu(zg_ref[pl.ds(r0, blk), :].astype(F32))
        o_ref[pl.ds(r0, blk), :] = o.astype(o_ref.dtype)
        return carry

    lax.fori_loop(0, t_len // blk, finish, 0)


def _delta_net(z, ab, t_len, layer, conv_w, a_log_row, dt_bias_row, dn_norm_w, state):
    m = z.shape[0]
    n_seq = m // t_len
    n_chunks = t_len // DN_CHUNK
    assert n_chunks % PRE_UNROLL == 0 and t_len % min(DN_BLOCK, t_len) == 0
    has_state = state is not None

    def zspec(off):
        return pl.BlockSpec((t_len, HEAD_DIM), lambda b, h: (b, off + h))

    def cwspec(off):
        return pl.BlockSpec((None, CONV_K, HEAD_DIM), lambda b, h: (layer, 0, off + h))

    row_spec = pl.BlockSpec((None, 1, 128), lambda b, h: (layer, 0, 0))
    in_specs = [zspec(OFF_DQ), zspec(OFF_DK), zspec(OFF_DV), zspec(OFF_DG),
                pl.BlockSpec((t_len, 128), lambda b, h: (b, 0)),
                cwspec(0), cwspec(DN_HEADS), cwspec(2 * DN_HEADS),
                row_spec, row_spec, row_spec]
    args = [z, z, z, z, ab, conv_w, conv_w, conv_w, a_log_row, dt_bias_row, dn_norm_w]
    if has_state:
        in_specs.append(pl.BlockSpec((None, None, 2, None, HEAD_DIM, HEAD_DIM),
                                     lambda b, h: (b, layer, 0, h, 0, 0)))
        args.append(state)
    c2 = 2 * DN_CHUNK
    group_set = [
        pltpu.VMEM((PRE_UNROLL, c2, HEAD_DIM), F32),
        pltpu.VMEM((PRE_UNROLL, 2 * c2, HEAD_DIM), BF16),
        pltpu.VMEM((PRE_UNROLL, c2, c2), BF16),
        pltpu.VMEM((PRE_UNROLL, HEAD_DIM, c2), BF16),
        pltpu.VMEM((PRE_UNROLL, 8, 2 * HEAD_DIM), F32),
        pltpu.VMEM((PRE_UNROLL, c2, 2 * HEAD_DIM), BF16),
        pltpu.VMEM((PRE_UNROLL, c2, c2), F32),
    ]
    scratch = [
        pltpu.VMEM((t_len + 16, HEAD_DIM), F32),
        pltpu.VMEM((t_len, HEAD_DIM), F32),
        pltpu.VMEM((t_len, HEAD_DIM), F32),
        pltpu.VMEM((t_len, HEAD_DIM), F32),
        pltpu.VMEM((HEAD_DIM, 2 * HEAD_DIM), F32),
        pltpu.VMEM((2, t_len, HEAD_DIM), F32),
    ] + group_set * 3
    return pl.pallas_call(
        functools.partial(_dn_kernel, t_len=t_len, has_state=has_state),
        grid=(n_seq, DN_HEADS),
        in_specs=in_specs,
        out_specs=[pl.BlockSpec((t_len, HEAD_DIM), lambda b, h: (b, h)),
                   pl.BlockSpec((None, 2, None, HEAD_DIM, HEAD_DIM), lambda b, h: (b, 0, h, 0, 0))],
        out_shape=[jax.ShapeDtypeStruct((m, DN_W), BF16),
                   jax.ShapeDtypeStruct((n_seq, 2, DN_HEADS, HEAD_DIM, HEAD_DIM), F32)],
        scratch_shapes=scratch,
        compiler_params=_params("arbitrary", "arbitrary"),
        name="delta_net",
    )(*args)


def _outproj_kernel(a1_ref, a2_ref, a3_ref, w_ref, o_ref):
    acc = jnp.dot(a1_ref[...], w_ref[0:DN_W, :], preferred_element_type=F32)
    acc += jnp.dot(a2_ref[...], w_ref[DN_W:DN_W + GQA_W, :], preferred_element_type=F32)
    acc += jnp.dot(a3_ref[...], w_ref[DN_W + GQA_W:MIX_W, :], preferred_element_type=F32)
    o_ref[...] = acc.astype(o_ref.dtype)


def _out_projection(dn, gqa, diff, w_out, layer):
    m = dn.shape[0]
    d = w_out.shape[-1]
    tm = min(1024, m)
    tn = min(1024, d)
    return pl.pallas_call(
        _outproj_kernel,
        grid=(m // tm, d // tn),
        in_specs=[pl.BlockSpec((tm, DN_W), lambda i, j: (i, 0)),
                  pl.BlockSpec((tm, GQA_W), lambda i, j: (i, 0)),
                  pl.BlockSpec((tm, DIFF_W), lambda i, j: (i, 0)),
                  pl.BlockSpec((None, MIX_W, tn), lambda i, j: (layer, 0, j))],
        out_specs=pl.BlockSpec((tm, tn), lambda i, j: (i, j)),
        out_shape=jax.ShapeDtypeStruct((m, d), BF16),
        compiler_params=_params("arbitrary", "arbitrary"),
        name="out_projection",
    )(dn, gqa, diff, w_out)


def _residual_kernel(x_ref, mix_ref, g_ref, pw_ref, o_ref):
    o_ref[...] = x_ref[...] + g_ref[...] * _rms(mix_ref[...].astype(F32), pw_ref[...])


def _post_residual(x, mix, mod, mod_row0, rows_per_mod, layer, post_w):
    m, d = x.shape
    tm = min(256, rows_per_mod)
    tiles_per_mod = rows_per_mod // tm
    base = layer * MOD_ROWS + mod_row0
    return pl.pallas_call(
        _residual_kernel,
        grid=(m // tm,),
        in_specs=[pl.BlockSpec((tm, d), lambda i: (i, 0)),
                  pl.BlockSpec((tm, d), lambda i: (i, 0)),
                  pl.BlockSpec((None, 1, d), lambda i: (base + i // tiles_per_mod, 0, 2)),
                  pl.BlockSpec((None, 1, d), lambda i: (layer, 0, 0))],
        out_specs=pl.BlockSpec((tm, d), lambda i: (i, 0)),
        out_shape=jax.ShapeDtypeStruct((m, d), F32),
        compiler_params=_params("arbitrary"),
        name="post_residual",
    )(x, mix, mod, post_w)


def _mixer_layer(x, t_len, mod, mod_row0, rows_per_mod, layer, lambda_init, wts, caches):
    (pre_w, post_w, w_main, w_ab, w_out, conv_w, a_log_row, dt_bias_row, dn_norm_w,
     q_norm_w, k_norm_w, diff_lambda, diff_norm_w) = wts
    latent = caches is not None
    z, ab = _in_projection(x, mod, mod_row0, rows_per_mod, layer, pre_w, w_main, w_ab)
    if latent:
        cgk, cgv, cfk, cfv, state = caches
        gkeys, fkeys = _key_prep(z, t_len, layer, k_norm_w, rope=True)
        fkey_col0 = 0
    else:
        cgk = cgv = cfk = cfv = state = None
        gkeys = _key_prep(z, t_len, layer, k_norm_w, rope=False)
        fkeys, fkey_col0 = z, OFF_FK // 2
    dn, s_fin = _delta_net(z, ab, t_len, layer, conv_w, a_log_row, dt_bias_row, dn_norm_w, state)
    gqa = _gqa_attention(z, gkeys, t_len, layer, q_norm_w, cgk, cgv, rope=latent)
    diff = _diff_attention(z, fkeys, fkey_col0, t_len, layer, lambda_init, diff_lambda, diff_norm_w,
                           cfk, cfv, rope=latent)
    mix = _out_projection(dn, gqa, diff, w_out, layer)
    y = _post_residual(x, mix, mod, mod_row0, rows_per_mod, layer, post_w)
    return y, z, gkeys, s_fin


def kernel(x_prompt, x_sample, cache_gqa_k, cache_gqa_v, cache_diff_k, cache_diff_v, state_dn, c, c_ctx,
           w_mod, b_mod, pre_norm_w, post_norm_w, w_in, w_out, dn_conv_w, dn_a_log, dn_dt_bias, dn_norm_w,
           gqa_q_norm_w, gqa_k_norm_w, diff_lambda, diff_norm_w):
    n_ctx, t_ctx, d = x_prompt.shape
    n_lat, t_lat, _ = x_sample.shape
    depth = w_in.shape[0]
    p_len = cache_gqa_k.shape[2]
    assert n_lat + 1 <= MOD_ROWS

    n_dn = 4 * DN_W
    w_main = jnp.concatenate([w_in[:, :, :n_dn], w_in[:, :, n_dn + N_GATE_COLS:]], axis=-1).astype(BF16)
    w_ab = jnp.pad(w_in[:, :, n_dn:n_dn + N_GATE_COLS], ((0, 0), (0, 0), (0, 128 - N_GATE_COLS))).astype(BF16)
    w_out_b = w_out.astype(BF16)
    row128 = lambda a: jnp.pad(a.reshape(depth, 1, -1), ((0, 0), (0, 0), (0, 128 - a[0].size)))
    a_log_row = row128(dn_a_log)
    dt_bias_row = row128(dn_dt_bias)
    r3 = lambda a: a.reshape(depth, 1, a.shape[-1])

    cond = jnp.concatenate([c_ctx[None, :], c, jnp.zeros((MOD_ROWS - 1 - n_lat, d), F32)], axis=0)
    mod = _modulation(cond, w_mod, b_mod).reshape(depth * MOD_ROWS, 1, 3 * d)

    cgk = cache_gqa_k.reshape(n_lat, depth, p_len, GQA_KV_W)
    cgv = cache_gqa_v.reshape(n_lat, depth, p_len, GQA_KV_W)
    cfk = cache_diff_k.reshape(n_lat, depth, p_len, DIFF_W)
    cfv = cache_diff_v.reshape(n_lat, depth, p_len, DIFF_W)

    xp = x_prompt.reshape(n_ctx * t_ctx, d)
    xs = x_sample.reshape(n_lat * t_lat, d)
    new_k, new_v, new_dk, new_dv, new_s = [], [], [], [], []
    for l in range(depth):
        lambda_init = 0.8 - 0.6 * math.exp(-0.3 * l)
        wts = (r3(pre_norm_w), r3(post_norm_w), w_main, w_ab, w_out_b, dn_conv_w, a_log_row, dt_bias_row,
               r3(dn_norm_w), r3(gqa_q_norm_w), r3(gqa_k_norm_w), diff_lambda, r3(diff_norm_w))
        xp, z, gk, s_fin = _mixer_layer(xp, t_ctx, mod, 0, n_ctx * t_ctx, l, lambda_init, wts, None)
        zc = lambda off, n: z[:, off * HEAD_DIM:off * HEAD_DIM + n].astype(F32)
        new_k.append(gk.reshape(n_ctx, t_ctx, GQA_KV_HEADS, HEAD_DIM))
        new_v.append(zc(OFF_GV, GQA_KV_W).reshape(n_ctx, t_ctx, GQA_KV_HEADS, HEAD_DIM))
        new_dk.append(zc(OFF_FK, DIFF_W).reshape(n_ctx, t_ctx, DIFF_HEADS, 2, HEAD_DIM))
        new_dv.append(zc(OFF_FV, DIFF_W).reshape(n_ctx, t_ctx, DIFF_HEADS, 2 * HEAD_DIM))
        new_s.append(s_fin)
        xs, _, _, _ = _mixer_layer(xs, t_lat, mod, 1, t_lat, l, lambda_init, wts,
                                   (cgk, cgv, cfk, cfv, state_dn))
    return (xp.reshape(n_ctx, t_ctx, d), xs.reshape(n_lat, t_lat, d),
            jnp.stack(new_k, axis=1), jnp.stack(new_v, axis=1), jnp.stack(new_dk, axis=1),
            jnp.stack(new_dv, axis=1), jnp.stack(new_s, axis=1))
```

```python
import functools
import math

import numpy as np
import jax
import jax.numpy as jnp
from jax import lax
from jax.experimental import pallas as pl
from jax.experimental.pallas import tpu as pltpu

F32 = jnp.float32
BF16 = jnp.bfloat16

HEAD_DIM = 128
GRID_W = 64
ROPE_THETA = 10000.0
NORM_EPS = 1e-6
DN_HEADS = 8
DN_CHUNK = 64
CONV_K = 3
GQA_Q_HEADS = 16
GQA_KV_HEADS = 4
GQA_GROUP = GQA_Q_HEADS // GQA_KV_HEADS
DIFF_HEADS = 4
DN_W = DN_HEADS * HEAD_DIM
GQA_W = GQA_Q_HEADS * HEAD_DIM
GQA_KV_W = GQA_KV_HEADS * HEAD_DIM
DIFF_W = DIFF_HEADS * 2 * HEAD_DIM
MIX_W = DN_W + GQA_W + DIFF_W
N_GATE_COLS = 4 * DN_HEADS

OFF_DQ, OFF_DK, OFF_DV, OFF_DG = 0, 8, 16, 24
OFF_GQ, OFF_GK, OFF_GV, OFF_GG = 32, 48, 52, 56
OFF_FQ, OFF_FK, OFF_FV, OFF_FG = 72, 80, 88, 96
Z_BLOCKS = 104
Z_COLS = Z_BLOCKS * HEAD_DIM

VMEM_LIMIT_BYTES = 56 * 1024 * 1024
MOD_ROWS = 16
LOG2E = math.log2(math.e)
PRE_UNROLL = 4
DN_BLOCK = 1024


def _params(*sem):
    return pltpu.CompilerParams(dimension_semantics=sem, vmem_limit_bytes=VMEM_LIMIT_BYTES)


def _silu(x):
    return x * jax.nn.sigmoid(x)


def _bdot(a, b):
    return jnp.dot(a.astype(BF16), b.astype(BF16), preferred_element_type=F32)


def _bdot_nt(a, b):
    return lax.dot_general(a.astype(BF16), b.astype(BF16), (((1,), (1,)), ((), ())),
                           preferred_element_type=F32)


def _mod_kernel(c_ref, w_ref, b_ref, o_ref):
    o_ref[...] = _bdot(_silu(c_ref[...]), w_ref[...]) + b_ref[...]


def _modulation(cond, w_mod, b_mod):
    n_layers, d, n = w_mod.shape
    tn = min(512, n)
    return pl.pallas_call(
        _mod_kernel,
        grid=(n_layers, n // tn),
        in_specs=[
            pl.BlockSpec((MOD_ROWS, d), lambda l, j: (0, 0)),
            pl.BlockSpec((None, d, tn), lambda l, j: (l, 0, j)),
            pl.BlockSpec((None, 1, tn), lambda l, j: (l, 0, j)),
        ],
        out_specs=pl.BlockSpec((None, MOD_ROWS, tn), lambda l, j: (l, 0, j)),
        out_shape=jax.ShapeDtypeStruct((n_layers, MOD_ROWS, n), F32),
        compiler_params=_params("arbitrary", "arbitrary"),
        name="modulation",
    )(cond, w_mod, b_mod.reshape(n_layers, 1, n))


def _inproj_kernel(x_ref, sh_ref, sc_ref, pw_ref, w_ref, wab_ref, z_ref, ab_ref, h_scr):
    @pl.when(pl.program_id(1) == 0)
    def _():
        x = x_ref[...]
        y = x * lax.rsqrt(jnp.mean(x * x, axis=-1, keepdims=True) + NORM_EPS) * pw_ref[...]
        h = (y * (1.0 + sc_ref[...]) + sh_ref[...]).astype(BF16)
        h_scr[...] = h
        ab_ref[...] = jnp.dot(h, wab_ref[...], preferred_element_type=F32)

    z_ref[...] = jnp.dot(h_scr[...], w_ref[...], preferred_element_type=F32).astype(z_ref.dtype)


def _in_projection(x, mod, mod_row0, rows_per_mod, layer, pre_w, w_main, w_ab):
    m, d = x.shape
    tm = min(512, rows_per_mod)
    tn = 1024
    assert rows_per_mod % tm == 0 and m % tm == 0 and Z_COLS % tn == 0
    tiles_per_mod = rows_per_mod // tm
    base = layer * MOD_ROWS + mod_row0

    def mod_map(part):
        return lambda i, j: (base + i // tiles_per_mod, 0, part)

    return pl.pallas_call(
        _inproj_kernel,
        grid=(m // tm, Z_COLS // tn),
        in_specs=[
            pl.BlockSpec((tm, d), lambda i, j: (i, 0)),
            pl.BlockSpec((None, 1, d), mod_map(0)),
            pl.BlockSpec((None, 1, d), mod_map(1)),
            pl.BlockSpec((None, 1, d), lambda i, j: (layer, 0, 0)),
            pl.BlockSpec((None, d, tn), lambda i, j: (layer, 0, j)),
            pl.BlockSpec((None, d, 128), lambda i, j: (layer, 0, 0)),
        ],
        out_specs=[
            pl.BlockSpec((tm, tn), lambda i, j: (i, j)),
            pl.BlockSpec((tm, 128), lambda i, j: (i, 0)),
        ],
        out_shape=[jax.ShapeDtypeStruct((m, Z_COLS), BF16), jax.ShapeDtypeStruct((m, 128), F32)],
        scratch_shapes=[pltpu.VMEM((tm, d), BF16)],
        compiler_params=_params("arbitrary", "arbitrary"),
        name="in_projection",
    )(x, mod, mod, pre_w, w_main, w_ab)


def _rope_tables(t_len):
    half = HEAD_DIM // 2
    t = np.arange(t_len)
    pos = np.stack([t // GRID_W, t % GRID_W], axis=-1).astype(np.float64)
    inv_freq = ROPE_THETA ** (-np.arange(0, half, 2, dtype=np.float64) / half)
    ang = pos[..., None] * inv_freq
    ang = np.concatenate([ang, ang], axis=-1).reshape(t_len, HEAD_DIM)
    first = (np.arange(HEAD_DIM) % half) < half // 2
    cos = np.cos(ang)
    sin = np.sin(ang)
    sin_up = np.where(first, -sin, 0.0)
    sin_dn = np.where(first, 0.0, sin)
    return tuple(jnp.asarray(a, F32) for a in (cos, sin_up, sin_dn))


def _rope(x, cos, sin_up, sin_dn):
    q = HEAD_DIM // 4
    return x * cos + pltpu.roll(x, HEAD_DIM - q, 1) * sin_up + pltpu.roll(x, q, 1) * sin_dn


def _rms(x, w):
    return x * lax.rsqrt(jnp.mean(x * x, axis=-1, keepdims=True) + NORM_EPS) * w


def _keyprep_kernel(*refs, rope, n_fk):
    if rope:
        gk_ref, fk_ref, kw_ref, cos_ref, su_ref, sd_ref, gko_ref, fko_ref = refs
        tabs = (cos_ref[...], su_ref[...], sd_ref[...])
    else:
        gk_ref, kw_ref, gko_ref = refs
    kw = kw_ref[...]
    for j in range(GQA_KV_HEADS):
        sl = slice(j * HEAD_DIM, (j + 1) * HEAD_DIM)
        k = _rms(gk_ref[:, sl].astype(F32), kw)
        if rope:
            k = _rope(k, *tabs)
        gko_ref[:, sl] = k.astype(gko_ref.dtype)
    if rope:
        for j in range(n_fk):
            sl = slice(j * HEAD_DIM, (j + 1) * HEAD_DIM)
            fko_ref[:, sl] = _rope(fk_ref[:, sl].astype(F32), *tabs).astype(fko_ref.dtype)


def _key_prep(z, t_len, layer, k_norm_w, rope):
    m = z.shape[0]
    tm = min(256, t_len)
    tiles_per_seq = t_len // tm
    kw_spec = pl.BlockSpec((None, 1, HEAD_DIM), lambda i: (layer, 0, 0))
    gk_spec = pl.BlockSpec((tm, GQA_KV_W), lambda i: (i, OFF_GK * HEAD_DIM // GQA_KV_W))
    if rope:
        tabs = _rope_tables(t_len)
        tab_spec = pl.BlockSpec((tm, HEAD_DIM), lambda i: (i % tiles_per_seq, 0))
        return pl.pallas_call(
            functools.partial(_keyprep_kernel, rope=True, n_fk=2 * DIFF_HEADS),
            grid=(m // tm,),
            in_specs=[gk_spec, pl.BlockSpec((tm, DIFF_W), lambda i: (i, OFF_FK * HEAD_DIM // DIFF_W)),
                      kw_spec, tab_spec, tab_spec, tab_spec],
            out_specs=[pl.BlockSpec((tm, GQA_KV_W), lambda i: (i, 0)),
                       pl.BlockSpec((tm, DIFF_W), lambda i: (i, 0))],
            out_shape=[jax.ShapeDtypeStruct((m, GQA_KV_W), BF16), jax.ShapeDtypeStruct((m, DIFF_W), BF16)],
            compiler_params=_params("arbitrary"),
            name="key_prep_rope",
        )(z, z, k_norm_w, *tabs)
    return pl.pallas_call(
        functools.partial(_keyprep_kernel, rope=False, n_fk=0),
        grid=(m // tm,),
        in_specs=[gk_spec, kw_spec],
        out_specs=pl.BlockSpec((tm, GQA_KV_W), lambda i: (i, 0)),
        out_shape=jax.ShapeDtypeStruct((m, GQA_KV_W), F32),
        compiler_params=_params("arbitrary"),
        name="key_prep",
    )(z, k_norm_w)


def _fill_vt(v_ref, vt_ref, tk):
    def body(c, carry):
        r0 = pl.multiple_of(c * tk, tk)
        vt_ref[c] = jnp.transpose(v_ref[pl.ds(r0, tk), :].astype(F32)).astype(BF16)
        return carry

    lax.fori_loop(0, v_ref.shape[0] // tk, body, 0)


def _attend_t(q, k_ref, vt_ref, ck_ref, cvt_ref, s_refs, m_ref, l_ref, acc_ref, tk,
              first_scores=None, after_loop=None, at_end=None, q_transposed=False):
    m_ref[...] = jnp.full(m_ref.shape, -jnp.inf, F32)
    l_ref[...] = jnp.zeros(l_ref.shape, F32)
    acc_ref[...] = jnp.zeros(acc_ref.shape, F32)
    s_a, s_b = s_refs
    n_chunks = k_ref.shape[0] // tk

    def scores(k, s_ref):
        if q_transposed:
            s_ref[0:k.shape[0], :] = jnp.dot(k.astype(BF16), q, preferred_element_type=F32)
        else:
            s_ref[0:k.shape[0], :] = _bdot_nt(k, q)

    def softmax_pv(s_ref, vt):
        n = vt.shape[1]
        m_prev = m_ref[...]
        m_new = jnp.maximum(m_prev, jnp.max(s_ref[0:n, :], axis=0, keepdims=True))
        alpha = jnp.exp2(m_prev - m_new)
        p = jnp.exp2(s_ref[0:n, :] - m_new)
        l_ref[...] = alpha * l_ref[...] + jnp.sum(p, axis=0, keepdims=True)
        acc_ref[...] = acc_ref[...] * alpha + jnp.dot(vt, p.astype(BF16), preferred_element_type=F32)
        m_ref[...] = m_new

    def k_chunk(c):
        return k_ref[pl.ds(pl.multiple_of(c * tk, tk), tk), :]

    if first_scores is None:
        scores(k_chunk(0), s_a)
    if n_chunks > 1:
        assert n_chunks % 2 == 0

        def body(i, carry):
            scores(k_chunk(2 * i + 1), s_b)
            softmax_pv(s_a, vt_ref[2 * i])
            scores(k_chunk(2 * i + 2), s_a)
            softmax_pv(s_b, vt_ref[2 * i + 1])
            return carry

        first_iter = 0
        if first_scores is not None:
            assert n_chunks >= 4
            scores(k_chunk(1), s_b)
            softmax_pv(first_scores, vt_ref[0])
            scores(k_chunk(2), s_a)
            softmax_pv(s_b, vt_ref[1])
            first_iter = 1
        lax.fori_loop(first_iter, n_chunks // 2 - 1, body, 0)
        if after_loop is not None:
            after_loop()
        scores(k_chunk(n_chunks - 1), s_b)
        softmax_pv(s_a, vt_ref[n_chunks - 2])
        if ck_ref is not None:
            scores(ck_ref[...], s_a)
        softmax_pv(s_b, vt_ref[n_chunks - 1])
        if ck_ref is not None:
            softmax_pv(s_a, cvt_ref[...])
    else:
        if after_loop is not None:
            after_loop()
        softmax_pv(s_a if first_scores is None else first_scores, vt_ref[0])
        if ck_ref is not None:
            scores(ck_ref[...], s_b)
            softmax_pv(s_b, cvt_ref[...])
    if at_end is not None:
        at_end()
    return acc_ref[...] / l_ref[...]


def _gqa_kernel(*refs, rope, cached, tq, tk):
    refs = list(refs)
    q_ref, qn_ref, g_ref, k_ref, v_ref, qw_ref = refs[:6]
    pos = 6
    ck_ref = cv_ref = cvt_s = None
    if cached:
        ck_ref, cv_ref = refs[pos:pos + 2]
        pos += 2
    if rope:
        tab_refs = refs[pos:pos + 6]
        pos += 6
    o_ref, q_scr, qn_scr, vt_s = refs[pos:pos + 4]
    pos += 4
    if cached:
        cvt_s = refs[pos]
        pos += 1
    s_a, s_b, s_c, m_ref, l_ref, acc_ref = refs[pos:]
    first = pl.program_id(2) == 0
    scale = HEAD_DIM ** -0.5 * LOG2E

    def prepare(src_ref, tabs, dst):
        qw = qw_ref[...]
        for j in range(GQA_GROUP):
            qj = _rms(src_ref[:, j * HEAD_DIM:(j + 1) * HEAD_DIM].astype(F32), qw)
            if rope:
                qj = _rope(qj, *(r[...] for r in tabs))
            dst[:, j * tq:(j + 1) * tq] = jnp.transpose(qj * scale).astype(BF16)

    def first_chunk_scores(q_src):
        s_c[...] = jnp.dot(k_ref[0:tk, :].astype(BF16), q_src[...], preferred_element_type=F32)

    @pl.when(first)
    def _():
        _fill_vt(v_ref, vt_s, tk)
        if cached:
            cvt_s[...] = jnp.transpose(cv_ref[...]).astype(BF16)
        prepare(q_ref, tab_refs[0:3] if rope else None, q_scr)
        first_chunk_scores(q_scr)

    @pl.when(jnp.logical_not(first))
    def _():
        q_scr[...] = qn_scr[...]

    o = jnp.transpose(_attend_t(q_scr[...], k_ref, vt_s, ck_ref, cvt_s, (s_a, s_b), m_ref, l_ref, acc_ref, tk,
                                first_scores=s_c, q_transposed=True,
                                after_loop=lambda: prepare(qn_ref, tab_refs[3:6] if rope else None, qn_scr),
                                at_end=lambda: first_chunk_scores(qn_scr)))
    for j in range(GQA_GROUP):
        sl = slice(j * HEAD_DIM, (j + 1) * HEAD_DIM)
        o_ref[:, sl] = (o[j * tq:(j + 1) * tq, :] * _silu(g_ref[:, sl].astype(F32))).astype(o_ref.dtype)


def _gqa_attention(z, keys, t_len, layer, q_norm_w, cache_k, cache_v, rope):
    m = z.shape[0]
    n_seq = m // t_len
    tq = min(256, t_len)
    tk = min(512, t_len)
    nq = t_len // tq
    cached = cache_k is not None
    qcol = OFF_GQ * HEAD_DIM // (GQA_GROUP * HEAD_DIM)
    gcol = OFF_GG * HEAD_DIM // (GQA_GROUP * HEAD_DIM)
    nxt = lambda i: jnp.minimum(i + 1, nq - 1)
    in_specs = [
        pl.BlockSpec((tq, GQA_GROUP * HEAD_DIM), lambda b, g, i: (b * nq + i, qcol + g)),
        pl.BlockSpec((tq, GQA_GROUP * HEAD_DIM), lambda b, g, i: (b * nq + nxt(i), qcol + g)),
        pl.BlockSpec((tq, GQA_GROUP * HEAD_DIM), lambda b, g, i: (b * nq + i, gcol + g)),
        pl.BlockSpec((t_len, HEAD_DIM), lambda b, g, i: (b, g)),
        pl.BlockSpec((t_len, HEAD_DIM), lambda b, g, i: (b, OFF_GV + g)),
        pl.BlockSpec((None, 1, HEAD_DIM), lambda b, g, i: (layer, 0, 0)),
    ]
    args = [z, z, z, keys, z, q_norm_w]
    rows = GQA_GROUP * tq
    scratch = [pltpu.VMEM((HEAD_DIM, rows), BF16)] * 2 + [pltpu.VMEM((t_len // tk, HEAD_DIM, tk), BF16)]
    if cached:
        p_len = cache_k.shape[2]
        spec = pl.BlockSpec((None, None, p_len, HEAD_DIM), lambda b, g, i: (b, layer, 0, g))
        in_specs += [spec, spec]
        args += [cache_k, cache_v]
        scratch.append(pltpu.VMEM((HEAD_DIM, p_len), BF16))
    if rope:
        spec = pl.BlockSpec((tq, HEAD_DIM), lambda b, g, i: (i, 0))
        spec_n = pl.BlockSpec((tq, HEAD_DIM), lambda b, g, i: (nxt(i), 0))
        in_specs += [spec, spec, spec, spec_n, spec_n, spec_n]
        args += 2 * list(_rope_tables(t_len))
    scratch += [pltpu.VMEM((tk, rows), F32)] * 3 + [
                pltpu.VMEM((1, rows), F32), pltpu.VMEM((1, rows), F32), pltpu.VMEM((HEAD_DIM, rows), F32)]
    return pl.pallas_call(
        functools.partial(_gqa_kernel, rope=rope, cached=cached, tq=tq, tk=tk),
        grid=(n_seq, GQA_KV_HEADS, nq),
        in_specs=in_specs,
        out_specs=pl.BlockSpec((tq, GQA_GROUP * HEAD_DIM), lambda b, g, i: (b * nq + i, g)),
        out_shape=jax.ShapeDtypeStruct((m, GQA_W), BF16),
        scratch_shapes=scratch,
        compiler_params=_params("arbitrary", "arbitrary", "arbitrary"),
        name="gqa_attention",
    )(*args)


def _diff_kernel(*refs, rope, cached, tk, lambda_init):
    refs = list(refs)
    q_ref, g_ref, k_ref, v_ref, lam_ref, nw_ref = refs[:6]
    pos = 6
    ck_ref = cv_ref = cvt_s = None
    if cached:
        ck_ref, cv_ref = refs[pos:pos + 2]
        pos += 2
    if rope:
        tabs = tuple(r[...] for r in refs[pos:pos + 3])
        pos += 3
    o_ref, vt_s = refs[pos:pos + 2]
    pos += 2
    if cached:
        cvt_s = refs[pos]
        pos += 1
    o1_s, s_a, s_b, m_ref, l_ref, acc_ref = refs[pos:]

    @pl.when(pl.program_id(2) == 0)
    def _():
        _fill_vt(v_ref, vt_s, tk)
        if cached:
            cvt_s[...] = jnp.transpose(cv_ref[...]).astype(BF16)

    scale = HEAD_DIM ** -0.5 * LOG2E
    lp = lam_ref[...]
    lam = (jnp.exp(jnp.sum(lp[0:1] * lp[1:2], axis=-1, keepdims=True))
           - jnp.exp(jnp.sum(lp[2:3] * lp[3:4], axis=-1, keepdims=True)) + lambda_init)
    for i in range(2):
        sl = slice(i * HEAD_DIM, (i + 1) * HEAD_DIM)
        qi = q_ref[:, sl].astype(F32)
        if rope:
            qi = _rope(qi, *tabs)
        qi = (qi * scale).astype(BF16)
        o_t = _attend_t(qi, k_ref.at[:, sl], vt_s, None if ck_ref is None else ck_ref.at[:, sl],
                        cvt_s, (s_a, s_b), m_ref, l_ref, acc_ref, tk)
        if i == 0:
            o1_s[...] = o_t
    o = jnp.transpose(o1_s[...] - lam * o_t)
    o = _rms(o, nw_ref[...]) * (1.0 - lambda_init)
    o_ref[...] = (o * _silu(g_ref[...].astype(F32))).astype(o_ref.dtype)


def _diff_attention(z, keys, key_col0, t_len, layer, lambda_init, diff_lambda, diff_norm_w,
                    cache_k, cache_v, rope):
    m = z.shape[0]
    n_seq = m // t_len
    tq = min(1024, t_len)
    tk = min(512, t_len)
    nq = t_len // tq
    hw = 2 * HEAD_DIM
    cached = cache_k is not None
    in_specs = [
        pl.BlockSpec((tq, hw), lambda b, h, i: (b * nq + i, OFF_FQ // 2 + h)),
        pl.BlockSpec((tq, hw), lambda b, h, i: (b * nq + i, OFF_FG // 2 + h)),
        pl.BlockSpec((t_len, hw), lambda b, h, i: (b, key_col0 + h)),
        pl.BlockSpec((t_len, hw), lambda b, h, i: (b, OFF_FV // 2 + h)),
        pl.BlockSpec((None, 4, HEAD_DIM), lambda b, h, i: (layer, 0, 0)),
        pl.BlockSpec((None, 1, hw), lambda b, h, i: (layer, 0, 0)),
    ]
    args = [z, z, keys, z, diff_lambda, diff_norm_w]
    scratch = [pltpu.VMEM((t_len // tk, hw, tk), BF16)]
    if cached:
        p_len = cache_k.shape[2]
        spec = pl.BlockSpec((None, None, p_len, hw), lambda b, h, i: (b, layer, 0, h))
        in_specs += [spec, spec]
        args += [cache_k, cache_v]
        scratch.append(pltpu.VMEM((hw, p_len), BF16))
    if rope:
        spec = pl.BlockSpec((tq, HEAD_DIM), lambda b, h, i: (i, 0))
        in_specs += [spec, spec, spec]
        args += list(_rope_tables(t_len))
    scratch += [pltpu.VMEM((hw, tq), F32), pltpu.VMEM((tk, tq), F32), pltpu.VMEM((tk, tq), F32),
                pltpu.VMEM((1, tq), F32), pltpu.VMEM((1, tq), F32), pltpu.VMEM((hw, tq), F32)]
    return pl.pallas_call(
        functools.partial(_diff_kernel, rope=rope, cached=cached, tk=tk, lambda_init=lambda_init),
        grid=(n_seq, DIFF_HEADS, nq),
        in_specs=in_specs,
        out_specs=pl.BlockSpec((tq, hw), lambda b, h, i: (b * nq + i, h)),
        out_shape=jax.ShapeDtypeStruct((m, DIFF_W), BF16),
        scratch_shapes=scratch,
        compiler_params=_params("arbitrary", "arbitrary", "arbitrary"),
        name="diff_attention",
    )(*args)


def _split_bf16(a):
    hi = a.astype(BF16)
    return hi, (a - hi.astype(F32)).astype(BF16)


def _mm3(a_parts, b_parts):
    a_hi, a_lo = a_parts
    b_hi, b_lo = b_parts
    return jnp.dot(jnp.concatenate([a_hi, a_hi, a_lo], axis=1), jnp.concatenate([b_hi, b_lo, b_hi], axis=0),
                   preferred_element_type=F32)


def _interleave(*gens):
    gens = list(gens)
    while gens:
        for g in list(gens):
            try:
                next(g)
            except StopIteration:
                gens.remove(g)


def _chunk_cumsum(x, reverse):
    n = x.shape[0]
    row = lax.broadcasted_iota(jnp.int32, x.shape, 0)
    s = 1
    while s < n:
        if reverse:
            x = x + jnp.where(row < n - s, pltpu.roll(x, n - s, 0), 0.0)
        else:
            x = x + jnp.where(row >= s, pltpu.roll(x, s, 0), 0.0)
        s *= 2
    return x


def _dn_kernel(*refs, t_len, has_state):
    refs = list(refs)
    (zq_ref, zk_ref, zv_ref, zg_ref, ab_ref, cwq_ref, cwk_ref, cwv_ref,
     alog_ref, dtb_ref, nw_ref) = refs[:11]
    pos = 11
    s0_ref = None
    if has_state:
        s0_ref = refs[pos]
        pos += 1
    o_ref, sfin_ref = refs[pos:pos + 2]
    raw_s, qn_s, kn_s, vn_s, s_s, o_s = refs[pos + 2:pos + 8]
    n_set = 7
    sets = [refs[pos + 8 + k * n_set:pos + 8 + (k + 1) * n_set] for k in range(3)]
    c_len = DN_CHUNK
    n_chunks = t_len // c_len
    n_groups = n_chunks // PRE_UNROLL
    pad = 8
    blk = min(DN_BLOCK, t_len)
    head = pl.program_id(1)

    def l2n(x):
        return x * lax.rsqrt(jnp.sum(x * x, axis=-1, keepdims=True) + NORM_EPS)

    zero_pad = jnp.zeros((pad, HEAD_DIM), F32)
    for src, cw_ref, dst, post in ((zq_ref, cwq_ref, qn_s, lambda x: l2n(x) * (HEAD_DIM ** -0.5)),
                                   (zk_ref, cwk_ref, kn_s, l2n), (zv_ref, cwv_ref, vn_s, lambda x: x)):
        raw_s[0:pad, :] = zero_pad
        raw_s[pad + t_len:2 * pad + t_len, :] = zero_pad

        def fill(b, carry, src=src):
            r0 = pl.multiple_of(b * blk, blk)
            raw_s[pl.ds(r0 + pad, blk), :] = src[pl.ds(r0, blk), :].astype(F32)
            return carry

        lax.fori_loop(0, t_len // blk, fill, 0)
        cw = cw_ref[...]

        def conv(b, carry, dst=dst, post=post, cw=cw):
            r0 = pl.multiple_of(b * blk, blk)
            n = blk + 2 * pad
            win = raw_s[pl.ds(r0, n), :]
            prev = pltpu.roll(win, 1, 0)[pad:pad + blk]
            nxt = pltpu.roll(win, n - 1, 0)[pad:pad + blk]
            y = _silu(prev * cw[0:1] + win[pad:pad + blk] * cw[1:2] + nxt * cw[2:3])
            dst[pl.ds(r0, blk), :] = post(y)
            return carry

        lax.fori_loop(0, t_len // blk, conv, 0)

    lane = lax.broadcasted_iota(jnp.int32, (c_len, 128), 1)
    row2 = lax.broadcasted_iota(jnp.int32, (2 * c_len, 2 * c_len), 0)
    col2 = lax.broadcasted_iota(jnp.int32, (2 * c_len, 2 * c_len), 1)
    fwd_blk = (row2 < c_len) & (col2 < c_len)
    bwd_blk = (row2 >= c_len) & (col2 >= c_len)
    incl = (fwd_blk & (row2 >= col2)) | (bwd_blk & (row2 <= col2))
    strict = (fwd_blk & (row2 > col2)) | (bwd_blk & (row2 < col2))
    eye = (row2 == col2).astype(F32)
    neg_a = -jnp.exp(alog_ref[...])
    dtb = dtb_ref[...]
    zeros_v = jnp.zeros((c_len, HEAD_DIM), BF16)

    def pick(x, col):
        return jnp.broadcast_to(jnp.sum(jnp.where(lane == col, x, 0.0), axis=-1, keepdims=True),
                                (c_len, HEAD_DIM))

    def rows_of(i):
        return (pl.multiple_of(i * c_len, c_len), pl.multiple_of((n_chunks - 1 - i) * c_len, c_len))

    def pair(ref, i):
        rf, rb = rows_of(i)
        return jnp.concatenate([ref[pl.ds(rf, c_len), :], ref[pl.ds(rb, c_len), :]], axis=0)

    def pre_instance(i, j, bufs):
        u_b, wq_b, qk_b, kdt_b, et_b, rhs_b, l_b = bufs
        rf, rb = rows_of(i)
        q2, k2, v2 = pair(qn_s, i), pair(kn_s, i), pair(vn_s, i)
        gs, betas = [], []
        for d, r0 in enumerate((rf, rb)):
            ab = ab_ref[pl.ds(r0, c_len), :]
            x = ab + dtb
            g_all = neg_a * (jnp.maximum(x, 0.0) + jnp.log1p(jnp.exp(-jnp.abs(x))))
            gs.append(pick(g_all, d * DN_HEADS + head))
            betas.append(pick(jax.nn.sigmoid(ab), (2 + d) * DN_HEADS + head))
        beta2 = jnp.concatenate(betas, axis=0)
        gc2 = jnp.concatenate([_chunk_cumsum(gs[0], False), _chunk_cumsum(gs[1], True)], axis=0)
        tots = [jnp.sum(g, axis=0, keepdims=True) for g in gs]
        tot2 = jnp.concatenate([jnp.broadcast_to(t, (c_len, HEAD_DIM)) for t in tots], axis=0)
        k2b = k2.astype(BF16)
        kk2 = _bdot_nt(k2b, k2b)
        qk2 = _bdot_nt(q2, k2b)
        decay = jnp.where(incl, jnp.exp(jnp.where(incl, gc2 - jnp.transpose(gc2), 0.0)), 0.0)
        e2 = jnp.exp(gc2)
        rhs_b[j] = jnp.concatenate([v2 * beta2, k2 * beta2 * e2], axis=-1).astype(BF16)
        qk_b[j] = (qk2 * decay).astype(BF16)
        wq_b[j, 2 * c_len:4 * c_len, :] = (q2 * e2).astype(BF16)
        kdt_b[j] = jnp.transpose(k2 * jnp.exp(tot2 - gc2)).astype(BF16)
        et_b[j] = jnp.concatenate([jnp.broadcast_to(jnp.exp(t), (8, HEAD_DIM)) for t in tots], axis=1)
        return jnp.where(strict, kk2 * beta2 * decay, 0.0)

    def prep_group(g, bufs):
        l_b = bufs[6]
        for j in range(PRE_UNROLL):
            l_b[j] = pre_instance(PRE_UNROLL * g + j, j, bufs)
            yield

    def solve_group(g, bufs):
        u_b, wq_b, qk_b, kdt_b, et_b, rhs_b, l_b = bufs
        l_mats = [l_b[j] for j in range(PRE_UNROLL)]
        ts = [eye - l for l in l_mats]
        l_parts = [_split_bf16(l) for l in l_mats]
        ps = [_mm3(lp, lp) for lp in l_parts]
        yield
        steps = int(math.log2(DN_CHUNK)) - 1
        for s in range(steps):
            p_parts = [_split_bf16(p) for p in ps]
            ts = [t + _mm3(_split_bf16(t), pp) for t, pp in zip(ts, p_parts)]
            yield
            if s + 1 < steps:
                ps = [_mm3(pp, pp) for pp in p_parts]
                yield
        for j in range(PRE_UNROLL):
            uw = jnp.dot(ts[j].astype(BF16), rhs_b[j], preferred_element_type=F32)
            u_b[j] = uw[:, 0:HEAD_DIM]
            wq_b[j, 0:2 * c_len, :] = uw[:, HEAD_DIM:].astype(BF16)
            yield

    def scan_group(g, bufs):
        u_b, wq_b, qk_b, kdt_b, et_b, rhs_b, l_b = bufs
        for j in range(PRE_UNROLL):
            rf, rb = rows_of(PRE_UNROLL * g + j)
            s2 = s_s[...]
            ws = jnp.dot(wq_b[j], s2.astype(BF16), preferred_element_type=F32)
            yield
            u2 = u_b[j]
            vb_f = (u2[0:c_len] - ws[0:c_len, 0:HEAD_DIM]).astype(BF16)
            vb_b = (u2[c_len:] - ws[c_len:2 * c_len, HEAD_DIM:]).astype(BF16)
            o2 = jnp.dot(qk_b[j], jnp.concatenate([vb_f, vb_b], axis=0), preferred_element_type=F32)
            o_s[0, pl.ds(rf, c_len), :] = ws[2 * c_len:3 * c_len, 0:HEAD_DIM] + o2[0:c_len]
            o_s[1, pl.ds(rb, c_len), :] = ws[3 * c_len:, HEAD_DIM:] + o2[c_len:]
            v_bd = jnp.concatenate([jnp.concatenate([vb_f, zeros_v], axis=1),
                                    jnp.concatenate([zeros_v, vb_b], axis=1)], axis=0)
            s_s[...] = s2 * et_b[j][0:1, :] + jnp.dot(kdt_b[j], v_bd, preferred_element_type=F32)
            yield

    if has_state:
        s_s[...] = jnp.concatenate([s0_ref[0], s0_ref[1]], axis=1).astype(F32)
    else:
        s_s[...] = jnp.zeros(s_s.shape, F32)

    def pipeline_step(h, r):
        gens = []
        for stage, dg in ((prep_group, 2), (solve_group, 1), (scan_group, 0)):
            if isinstance(h, int) and not 0 <= h + dg < n_groups:
                continue
            gens.append(stage(h + dg, sets[(r + dg) % 3]))
        _interleave(*gens)

    for h in range(-2, min(0, n_groups)):
        pipeline_step(h, h % 3)
    n_full = max(n_groups - 2, 0)

    def body(t, carry):
        for r in range(3):
            pipeline_step(3 * t + r, r)
        return carry

    lax.fori_loop(0, n_full // 3, body, 0)
    for h in range(n_full - n_full % 3, n_groups):
        pipeline_step(h, h % 3)
    s_fin = s_s[...]
    sfin_ref[0] = s_fin[:, 0:HEAD_DIM].astype(sfin_ref.dtype)
    sfin_ref[1] = s_fin[:, HEAD_DIM:].astype(sfin_ref.dtype)

    nw = nw_ref[...]

    def finish(b, carry):
        r0 = pl.multiple_of(b * blk, blk)
        o = o_s[0, pl.ds(r0, blk), :] + o_s[1, pl.ds(r0, blk), :]
        o = _rms(o, nw) * _silu(zg_ref[pl.ds(r0, blk), :].astype(F32))
        o_ref[pl.ds(r0, blk), :] = o.astype(o_ref.dtype)
        return carry

    lax.fori_loop(0, t_len // blk, finish, 0)


def _delta_net(z, ab, t_len, layer, conv_w, a_log_row, dt_bias_row, dn_norm_w, state):
    m = z.shape[0]
    n_seq = m // t_len
    n_chunks = t_len // DN_CHUNK
    assert n_chunks % PRE_UNROLL == 0 and t_len % min(DN_BLOCK, t_len) == 0
    has_state = state is not None

    def zspec(off):
        return pl.BlockSpec((t_len, HEAD_DIM), lambda b, h: (b, off + h))

    def cwspec(off):
        return pl.BlockSpec((None, CONV_K, HEAD_DIM), lambda b, h: (layer, 0, off + h))

    row_spec = pl.BlockSpec((None, 1, 128), lambda b, h: (layer, 0, 0))
    in_specs = [zspec(OFF_DQ), zspec(OFF_DK), zspec(OFF_DV), zspec(OFF_DG),
                pl.BlockSpec((t_len, 128), lambda b, h: (b, 0)),
                cwspec(0), cwspec(DN_HEADS), cwspec(2 * DN_HEADS),
                row_spec, row_spec, row_spec]
    args = [z, z, z, z, ab, conv_w, conv_w, conv_w, a_log_row, dt_bias_row, dn_norm_w]
    if has_state:
        in_specs.append(pl.BlockSpec((None, None, 2, None, HEAD_DIM, HEAD_DIM),
                                     lambda b, h: (b, layer, 0, h, 0, 0)))
        args.append(state)
    c2 = 2 * DN_CHUNK
    group_set = [
        pltpu.VMEM((PRE_UNROLL, c2, HEAD_DIM), F32),
        pltpu.VMEM((PRE_UNROLL, 2 * c2, HEAD_DIM), BF16),
        pltpu.VMEM((PRE_UNROLL, c2, c2), BF16),
        pltpu.VMEM((PRE_UNROLL, HEAD_DIM, c2), BF16),
        pltpu.VMEM((PRE_UNROLL, 8, 2 * HEAD_DIM), F32),
        pltpu.VMEM((PRE_UNROLL, c2, 2 * HEAD_DIM), BF16),
        pltpu.VMEM((PRE_UNROLL, c2, c2), F32),
    ]
    scratch = [
        pltpu.VMEM((t_len + 16, HEAD_DIM), F32),
        pltpu.VMEM((t_len, HEAD_DIM), F32),
        pltpu.VMEM((t_len, HEAD_DIM), F32),
        pltpu.VMEM((t_len, HEAD_DIM), F32),
        pltpu.VMEM((HEAD_DIM, 2 * HEAD_DIM), F32),
        pltpu.VMEM((2, t_len, HEAD_DIM), F32),
    ] + group_set * 3
    return pl.pallas_call(
        functools.partial(_dn_kernel, t_len=t_len, has_state=has_state),
        grid=(n_seq, DN_HEADS),
        in_specs=in_specs,
        out_specs=[pl.BlockSpec((t_len, HEAD_DIM), lambda b, h: (b, h)),
                   pl.BlockSpec((None, 2, None, HEAD_DIM, HEAD_DIM), lambda b, h: (b, 0, h, 0, 0))],
        out_shape=[jax.ShapeDtypeStruct((m, DN_W), BF16),
                   jax.ShapeDtypeStruct((n_seq, 2, DN_HEADS, HEAD_DIM, HEAD_DIM), F32)],
        scratch_shapes=scratch,
        compiler_params=_params("arbitrary", "arbitrary"),
        name="delta_net",
    )(*args)


def _outproj_kernel(a1_ref, a2_ref, a3_ref, w_ref, o_ref):
    acc = jnp.dot(a1_ref[...], w_ref[0:DN_W, :], preferred_element_type=F32)
    acc += jnp.dot(a2_ref[...], w_ref[DN_W:DN_W + GQA_W, :], preferred_element_type=F32)
    acc += jnp.dot(a3_ref[...], w_ref[DN_W + GQA_W:MIX_W, :], preferred_element_type=F32)
    o_ref[...] = acc.astype(o_ref.dtype)


def _out_projection(dn, gqa, diff, w_out, layer):
    m = dn.shape[0]
    d = w_out.shape[-1]
    tm = min(1024, m)
    tn = min(1024, d)
    return pl.pallas_call(
        _outproj_kernel,
        grid=(m // tm, d // tn),
        in_specs=[pl.BlockSpec((tm, DN_W), lambda i, j: (i, 0)),
                  pl.BlockSpec((tm, GQA_W), lambda i, j: (i, 0)),
                  pl.BlockSpec((tm, DIFF_W), lambda i, j: (i, 0)),
                  pl.BlockSpec((None, MIX_W, tn), lambda i, j: (layer, 0, j))],
        out_specs=pl.BlockSpec((tm, tn), lambda i, j: (i, j)),
        out_shape=jax.ShapeDtypeStruct((m, d), BF16),
        compiler_params=_params("arbitrary", "arbitrary"),
        name="out_projection",
    )(dn, gqa, diff, w_out)


def _residual_kernel(x_ref, mix_ref, g_ref, pw_ref, o_ref):
    o_ref[...] = x_ref[...] + g_ref[...] * _rms(mix_ref[...].astype(F32), pw_ref[...])


def _post_residual(x, mix, mod, mod_row0, rows_per_mod, layer, post_w):
    m, d = x.shape
    tm = min(256, rows_per_mod)
    tiles_per_mod = rows_per_mod // tm
    base = layer * MOD_ROWS + mod_row0
    return pl.pallas_call(
        _residual_kernel,
        grid=(m // tm,),
        in_specs=[pl.BlockSpec((tm, d), lambda i: (i, 0)),
                  pl.BlockSpec((tm, d), lambda i: (i, 0)),
                  pl.BlockSpec((None, 1, d), lambda i: (base + i // tiles_per_mod, 0, 2)),
                  pl.BlockSpec((None, 1, d), lambda i: (layer, 0, 0))],
        out_specs=pl.BlockSpec((tm, d), lambda i: (i, 0)),
        out_shape=jax.ShapeDtypeStruct((m, d), F32),
        compiler_params=_params("arbitrary"),
        name="post_residual",
    )(x, mix, mod, post_w)


def _mixer_layer(x, t_len, mod, mod_row0, rows_per_mod, layer, lambda_init, wts, caches):
    (pre_w, post_w, w_main, w_ab, w_out, conv_w, a_log_row, dt_bias_row, dn_norm_w,
     q_norm_w, k_norm_w, diff_lambda, diff_norm_w) = wts
    latent = caches is not None
    z, ab = _in_projection(x, mod, mod_row0, rows_per_mod, layer, pre_w, w_main, w_ab)
    if latent:
        cgk, cgv, cfk, cfv, state = caches
        gkeys, fkeys = _key_prep(z, t_len, layer, k_norm_w, rope=True)
        fkey_col0 = 0
    else:
        cgk = cgv = cfk = cfv = state = None
        gkeys = _key_prep(z, t_len, layer, k_norm_w, rope=False)
        fkeys, fkey_col0 = z, OFF_FK // 2
    dn, s_fin = _delta_net(z, ab, t_len, layer, conv_w, a_log_row, dt_bias_row, dn_norm_w, state)
    gqa = _gqa_attention(z, gkeys, t_len, layer, q_norm_w, cgk, cgv, rope=latent)
    diff = _diff_attention(z, fkeys, fkey_col0, t_len, layer, lambda_init, diff_lambda, diff_norm_w,
                           cfk, cfv, rope=latent)
    mix = _out_projection(dn, gqa, diff, w_out, layer)
    y = _post_residual(x, mix, mod, mod_row0, rows_per_mod, layer, post_w)
    return y, z, gkeys, s_fin


def kernel(x_prompt, x_sample, cache_gqa_k, cache_gqa_v, cache_diff_k, cache_diff_v, state_dn, c, c_ctx,
           w_mod, b_mod, pre_norm_w, post_norm_w, w_in, w_out, dn_conv_w, dn_a_log, dn_dt_bias, dn_norm_w,
           gqa_q_norm_w, gqa_k_norm_w, diff_lambda, diff_norm_w):
    n_ctx, t_ctx, d = x_prompt.shape
    n_lat, t_lat, _ = x_sample.shape
    depth = w_in.shape[0]
    p_len = cache_gqa_k.shape[2]
    assert n_lat + 1 <= MOD_ROWS

    n_dn = 4 * DN_W
    w_main = jnp.concatenate([w_in[:, :, :n_dn], w_in[:, :, n_dn + N_GATE_COLS:]], axis=-1).astype(BF16)
    w_ab = jnp.pad(w_in[:, :, n_dn:n_dn + N_GATE_COLS], ((0, 0), (0, 0), (0, 128 - N_GATE_COLS))).astype(BF16)
    w_out_b = w_out.astype(BF16)
    row128 = lambda a: jnp.pad(a.reshape(depth, 1, -1), ((0, 0), (0, 0), (0, 128 - a[0].size)))
    a_log_row = row128(dn_a_log)
    dt_bias_row = row128(dn_dt_bias)
    r3 = lambda a: a.reshape(depth, 1, a.shape[-1])

    cond = jnp.concatenate([c_ctx[None, :], c, jnp.zeros((MOD_ROWS - 1 - n_lat, d), F32)], axis=0)
    mod = _modulation(cond, w_mod, b_mod).reshape(depth * MOD_ROWS, 1, 3 * d)

    cgk = cache_gqa_k.reshape(n_lat, depth, p_len, GQA_KV_W)
    cgv = cache_gqa_v.reshape(n_lat, depth, p_len, GQA_KV_W)
    cfk = cache_diff_k.reshape(n_lat, depth, p_len, DIFF_W)
    cfv = cache_diff_v.reshape(n_lat, depth, p_len, DIFF_W)

    xp = x_prompt.reshape(n_ctx * t_ctx, d)
    xs = x_sample.reshape(n_lat * t_lat, d)
    new_k, new_v, new_dk, new_dv, new_s = [], [], [], [], []
    for l in range(depth):
        lambda_init = 0.8 - 0.6 * math.exp(-0.3 * l)
        wts = (r3(pre_norm_w), r3(post_norm_w), w_main, w_ab, w_out_b, dn_conv_w, a_log_row, dt_bias_row,
               r3(dn_norm_w), r3(gqa_q_norm_w), r3(gqa_k_norm_w), diff_lambda, r3(diff_norm_w))
        xp, z, gk, s_fin = _mixer_layer(xp, t_ctx, mod, 0, n_ctx * t_ctx, l, lambda_init, wts, None)
        zc = lambda off, n: z[:, off * HEAD_DIM:off * HEAD_DIM + n].astype(F32)
        new_k.append(gk.reshape(n_ctx, t_ctx, GQA_KV_HEADS, HEAD_DIM))
        new_v.append(zc(OFF_GV, GQA_KV_W).reshape(n_ctx, t_ctx, GQA_KV_HEADS, HEAD_DIM))
        new_dk.append(zc(OFF_FK, DIFF_W).reshape(n_ctx, t_ctx, DIFF_HEADS, 2, HEAD_DIM))
        new_dv.append(zc(OFF_FV, DIFF_W).reshape(n_ctx, t_ctx, DIFF_HEADS, 2 * HEAD_DIM))
        new_s.append(s_fin)
        xs, _, _, _ = _mixer_layer(xs, t_lat, mod, 1, t_lat, l, lambda_init, wts,
                                   (cgk, cgv, cfk, cfv, state_dn))
    return (xp.reshape(n_ctx, t_ctx, d), xs.reshape(n_lat, t_lat, d),
            jnp.stack(new_k, axis=1), jnp.stack(new_v, axis=1), jnp.stack(new_dk, axis=1),
            jnp.stack(new_dv, axis=1), jnp.stack(new_s, axis=1))
```
